```python
import math
import jax, jax.numpy as jnp
from jax import lax
import numpy as np

D_MODEL = 2048
BATCH = 2
SEQ = 4096
DEPTH = 2

N_A_LAYERS = DEPTH // 2
N_B_LAYERS = DEPTH - N_A_LAYERS
CONV_WIDTH = 31
CONV_DIM = D_MODEL
HEAD_DIM = 128
N_HEADS = D_MODEL // HEAD_DIM
ATTN_DIM = N_HEADS * HEAD_DIM
DILATED_GROUPS = ((128, 1), (512, 4), (2048, 16))
N_GROUPS = len(DILATED_GROUPS)
BLOCK = 128
N_BUCKETS = 32
MAX_EXACT = N_BUCKETS // 2
MAX_DISTANCE = 2048
EPS = 1e-6

kernel_name = "yoco_conformer_dilated_hybrid"


def rmsnorm(x, g):
    xf = x.astype(jnp.float32)
    y = xf * lax.rsqrt(jnp.mean(xf * xf, axis=-1, keepdims=True) + EPS)
    return (y * g.astype(jnp.float32)).astype(x.dtype)


def t5_bucket(dist):
    large = MAX_EXACT + (jnp.log(jnp.maximum(dist, 1).astype(jnp.float32) / MAX_EXACT)
                         / math.log(MAX_DISTANCE / MAX_EXACT)
                         * (N_BUCKETS - MAX_EXACT)).astype(jnp.int32)
    large = jnp.minimum(large, N_BUCKETS - 1)
    return jnp.where(dist < MAX_EXACT, dist, large)


def conformer_mixer(h, w_in, conv_w, conv_b, ln_g, ln_b, w_out):
    proj = jnp.einsum('bsd,de->bse', h, w_in)
    a, b, z = jnp.split(proj, 3, axis=-1)
    u = a * jax.nn.sigmoid(b)
    c = lax.conv_general_dilated(u, conv_w[:, None, :].astype(u.dtype), (1,),
                                 [(CONV_WIDTH - 1, 0)],
                                 dimension_numbers=('NWC', 'WIO', 'NWC'),
                                 feature_group_count=CONV_DIM) + conv_b
    cf = c.astype(jnp.float32)
    mu = jnp.mean(cf, axis=-1, keepdims=True)
    var = jnp.mean(jnp.square(cf - mu), axis=-1, keepdims=True)
    cn = (cf - mu) * lax.rsqrt(var + EPS) * ln_g.astype(jnp.float32) + ln_b.astype(jnp.float32)
    y = jax.nn.silu(cn).astype(h.dtype) * jax.nn.silu(z)
    return jnp.einsum('bse,ed->bsd', y, w_out)


def dilated_branch(q, k, v, window, dil, rel_bias):
    B, S, H, Dh = q.shape
    L = S // dil
    w_sub = window // dil
    nb = -(-L // BLOCK)
    Lp = nb * BLOCK
    N = B * dil

    def to_blocks(t):
        t = t.reshape(B, L, dil, H, Dh).transpose(0, 2, 1, 3, 4).reshape(N, L, H, Dh)
        t = jnp.pad(t, ((0, 0), (0, Lp - L), (0, 0), (0, 0)))
        return t.reshape(N, nb, BLOCK, H, Dh)

    def with_prev(t):
        prev = jnp.pad(t[:, :-1], ((0, 0), (1, 0), (0, 0), (0, 0), (0, 0)))
        return jnp.concatenate([prev, t], axis=2)

    qb = to_blocks(q).astype(jnp.float32)
    kk = with_prev(to_blocks(k)).astype(jnp.float32)
    vv = with_prev(to_blocks(v)).astype(jnp.float32)

    delta = (jnp.arange(BLOCK)[:, None] + BLOCK) - jnp.arange(2 * BLOCK)[None, :]
    local = (delta >= 0) & (delta <= w_sub)
    key_pos = jnp.arange(nb)[:, None] * BLOCK - BLOCK + jnp.arange(2 * BLOCK)[None, :]
    mask = local[None] & (key_pos >= 0)[:, None, :]
    bias = rel_bias[t5_bucket(jnp.clip(delta, 0) * dil)]
    bias = bias.astype(jnp.float32).transpose(2, 0, 1)

    s = jnp.einsum('nbqhd,nbkhd->nbhqk', qb, kk) * (HEAD_DIM ** -0.5) + bias
    s = jnp.where(mask[None, :, None], s, -jnp.inf)
    m = jnp.max(s, axis=-1, keepdims=True)
    p = jnp.exp(s - m)
    den = jnp.sum(p, axis=-1, keepdims=True)
    o = jnp.einsum('nbhqk,nbkhd->nbqhd', p, vv) / den.transpose(0, 1, 3, 2, 4)
    lse = (m + jnp.log(den))[..., 0].transpose(0, 1, 3, 2)

    def from_blocks(t):
        rest = t.shape[3:]
        t = t.reshape(N, Lp, *rest)[:, :L].reshape(B, dil, L, *rest)
        return jnp.swapaxes(t, 1, 2).reshape(B, S, *rest)

    return from_blocks(o), from_blocks(lse)


def shared_kv(x, kv_norm, w_kv):
    B, S, _ = x.shape
    kv = jnp.einsum('bsd,de->bse', rmsnorm(x, kv_norm), w_kv)
    parts = jnp.split(kv, 2 * N_GROUPS, axis=-1)
    return [(parts[2 * g].reshape(B, S, N_HEADS, HEAD_DIM),
             parts[2 * g + 1].reshape(B, S, N_HEADS, HEAD_DIM)) for g in range(N_GROUPS)]


def dilated_mixer(h, w_in, w_out, kv, rel_bias):
    B, S, _ = h.shape
    parts = jnp.split(jnp.einsum('bsd,de->bse', h, w_in), N_GROUPS + 1, axis=-1)
    outs, lses = [], []
    for g, (window, dil) in enumerate(DILATED_GROUPS):
        q = parts[g].reshape(B, S, N_HEADS, HEAD_DIM)
        o, lse = dilated_branch(q, kv[g][0], kv[g][1], window, dil, rel_bias)
        outs.append(o)
        lses.append(lse)
    alpha = jax.nn.softmax(jnp.stack(lses), axis=0)
    o = jnp.sum(alpha[..., None] * jnp.stack(outs), axis=0).reshape(B, S, ATTN_DIM)
    y = o.astype(h.dtype) * jax.nn.silu(parts[-1])
    return jnp.einsum('bse,ed->bsd', y, w_out)


def setup_inputs(seed: int = 0) -> dict:
    key = jax.random.key(seed)
    ks = jax.random.split(key, 16)
    f32 = jnp.float32
    D, E = D_MODEL, CONV_DIM
    nrm = lambda k, shape, scale: jax.random.normal(k, shape, f32) * scale
    return {
        "x": nrm(ks[0], (BATCH, SEQ, D), 1.0),
        "a_norm": 1.0 + nrm(ks[1], (N_A_LAYERS, D), 0.02),
        "a_w_in": nrm(ks[2], (N_A_LAYERS, D, 3 * E), D ** -0.5),
        "a_conv_w": nrm(ks[3], (N_A_LAYERS, CONV_WIDTH, E), CONV_WIDTH ** -0.5),
        "a_conv_b": nrm(ks[4], (N_A_LAYERS, E), 0.02),
        "a_ln_g": 1.0 + nrm(ks[5], (N_A_LAYERS, E), 0.02),
        "a_ln_b": nrm(ks[6], (N_A_LAYERS, E), 0.02),
        "a_w_out": nrm(ks[7], (N_A_LAYERS, E, D), E ** -0.5),
        "kv_norm": 1.0 + nrm(ks[8], (D,), 0.02),
        "w_kv": nrm(ks[9], (D, 2 * N_GROUPS * ATTN_DIM), D ** -0.5),
        "b_norm": 1.0 + nrm(ks[10], (N_B_LAYERS, D), 0.02),
        "b_w_in": nrm(ks[11], (N_B_LAYERS, D, (N_GROUPS + 1) * ATTN_DIM), D ** -0.5),
        "b_w_out": nrm(ks[12], (N_B_LAYERS, ATTN_DIM, D), ATTN_DIM ** -0.5),
        "rel_bias": nrm(ks[13], (N_BUCKETS, N_HEADS), 0.2),
        "final_norm": 1.0 + nrm(ks[14], (D,), 0.02),
    }


def reference(x, a_norm, a_w_in, a_conv_w, a_conv_b, a_ln_g, a_ln_b, a_w_out,
              kv_norm, w_kv, b_norm, b_w_in, b_w_out, rel_bias, final_norm):
    kv = None
    for layer in range(DEPTH):
        if layer < N_A_LAYERS:
            i = layer
            x = x + conformer_mixer(rmsnorm(x, a_norm[i]), a_w_in[i], a_conv_w[i], a_conv_b[i],
                                    a_ln_g[i], a_ln_b[i], a_w_out[i])
        else:
            if kv is None:
                kv = shared_kv(x, kv_norm, w_kv)
            j = layer - N_A_LAYERS
            x = x + dilated_mixer(rmsnorm(x, b_norm[j]), b_w_in[j], b_w_out[j], kv, rel_bias)
    return rmsnorm(x, final_norm)
```

```python
import functools
import math

import numpy as np
import jax
import jax.numpy as jnp
from jax import lax
from jax.experimental import pallas as pl
from jax.experimental.pallas import tpu as pltpu

D_MODEL = 2048
CONV_WIDTH = 31
HEAD_DIM = 128
N_HEADS = D_MODEL // HEAD_DIM
DILATED_GROUPS = ((128, 1), (512, 4), (2048, 16))
N_GROUPS = len(DILATED_GROUPS)
BLOCK = 128
N_BUCKETS = 32
MAX_EXACT = N_BUCKETS // 2
MAX_DISTANCE = 2048
EPS = 1e-6
HALO = 32

F32 = jnp.float32
BF16 = jnp.bfloat16
VMEM_LIMIT = 56 * 1024 * 1024


def _params(sem):
    return pltpu.CompilerParams(dimension_semantics=sem, vmem_limit_bytes=VMEM_LIMIT)


def _rmsnorm_kernel(x_ref, g_ref, o_ref):
    x = x_ref[...]
    ms = jnp.mean(x * x, axis=-1, keepdims=True)
    o_ref[...] = (x * lax.rsqrt(ms + EPS) * g_ref[...]).astype(o_ref.dtype)


def _rmsnorm_bf16(x2d, g, tm=512):
    M, D = x2d.shape
    return pl.pallas_call(
        _rmsnorm_kernel,
        grid=(M // tm,),
        in_specs=[pl.BlockSpec((tm, D), lambda i: (i, 0)),
                  pl.BlockSpec((1, D), lambda i: (0, 0))],
        out_specs=pl.BlockSpec((tm, D), lambda i: (i, 0)),
        out_shape=jax.ShapeDtypeStruct((M, D), BF16),
        compiler_params=_params(("arbitrary",)),
    )(x2d, g.reshape(1, D))


def _cast_kernel(w_ref, o_ref):
    o_ref[...] = w_ref[...].astype(o_ref.dtype)


def _cast_bf16(w, tr=512):
    R, C = w.shape
    return pl.pallas_call(
        _cast_kernel,
        grid=(R // tr,),
        in_specs=[pl.BlockSpec((tr, C), lambda i: (i, 0))],
        out_specs=pl.BlockSpec((tr, C), lambda i: (i, 0)),
        out_shape=jax.ShapeDtypeStruct((R, C), BF16),
        compiler_params=_params(("arbitrary",)),
    )(w)


def _inproj_a_kernel(h_ref, wa_ref, wb_ref, wz_ref, u_ref, sz_ref, wa_s, wb_s, wz_s):
    @pl.when(pl.program_id(1) == 0)
    def _():
        wa_s[...] = wa_ref[...].astype(BF16)
        wb_s[...] = wb_ref[...].astype(BF16)
        wz_s[...] = wz_ref[...].astype(BF16)

    h = h_ref[...]
    a = jnp.dot(h, wa_s[...], preferred_element_type=F32)
    b = jnp.dot(h, wb_s[...], preferred_element_type=F32)
    u_ref[...] = a * jax.nn.sigmoid(b)
    z = jnp.dot(h, wz_s[...], preferred_element_type=F32)
    sz_ref[...] = z * jax.nn.sigmoid(z)


def _inproj_a(h, w_in, tm=1024, tn=512):
    M, D = h.shape
    E = w_in.shape[1] // 3
    nb = E // tn
    return pl.pallas_call(
        _inproj_a_kernel,
        grid=(nb, M // tm),
        in_specs=[pl.BlockSpec((tm, D), lambda n, m: (m, 0)),
                  pl.BlockSpec((D, tn), lambda n, m: (0, n)),
                  pl.BlockSpec((D, tn), lambda n, m: (0, n + nb)),
                  pl.BlockSpec((D, tn), lambda n, m: (0, n + 2 * nb))],
        out_specs=[pl.BlockSpec((tm, tn), lambda n, m: (m, n)),
                   pl.BlockSpec((tm, tn), lambda n, m: (m, n))],
        out_shape=[jax.ShapeDtypeStruct((M, E), F32),
                   jax.ShapeDtypeStruct((M, E), F32)],
        scratch_shapes=[pltpu.VMEM((D, tn), BF16)] * 3,
        compiler_params=_params(("arbitrary", "arbitrary")),
    )(h, w_in, w_in, w_in)


def _conv_out_kernel(ucur_ref, uprev_ref, sz_ref, x_ref, cw_ref, cb_ref, lg_ref, lb_ref,
                     wo_ref, gkv_ref, gb_ref, x1_ref, hk_ref, hb_ref, win_s, c_s, *, tm, rc):
    E = ucur_ref.shape[-1]
    first = pl.program_id(1) == 0
    win_s[0:HALO, :] = jnp.where(first, 0.0, uprev_ref[0])
    win_s[HALO:HALO + tm, :] = ucur_ref[0]

    base = HALO - (CONV_WIDTH - 1)
    for ct in range(E // 128):
        cs = slice(ct * 128, (ct + 1) * 128)
        wv = [cw_ref[j:j + 1, cs] for j in range(CONV_WIDTH)]
        bias = cb_ref[:, cs]
        for r in range(tm // rc):
            r0 = r * rc
            acc = win_s[r0 + base:r0 + base + rc, cs] * wv[0]
            for j in range(1, CONV_WIDTH):
                acc = acc + win_s[r0 + base + j:r0 + base + j + rc, cs] * wv[j]
            c_s[r0:r0 + rc, cs] = acc + bias

    c = c_s[...]
    mu = jnp.mean(c, axis=-1, keepdims=True)
    cc = c - mu
    var = jnp.mean(cc * cc, axis=-1, keepdims=True)
    cn = cc * lax.rsqrt(var + EPS) * lg_ref[...] + lb_ref[...]
    y = (cn * jax.nn.sigmoid(cn)) * sz_ref[0]
    o = jnp.dot(y.astype(BF16), wo_ref[...], preferred_element_type=F32)
    x1 = x_ref[0] + o
    x1_ref[0] = x1
    xn = x1 * lax.rsqrt(jnp.mean(x1 * x1, axis=-1, keepdims=True) + EPS)
    hk_ref[0] = (xn * gkv_ref[...]).astype(BF16)
    hb_ref[0] = (xn * gb_ref[...]).astype(BF16)


def _conv_out(u, sz, x, conv_w, conv_b, ln_g, ln_b, w_out_bf16, g_kv, g_b, tm=256, rc=32):
    B, S, E = u.shape
    D = x.shape[-1]
    row = lambda a: a.reshape(1, -1)
    full = lambda shape: pl.BlockSpec(shape, lambda b, i: (0,) * len(shape))
    tile = lambda w: pl.BlockSpec((1, tm, w), lambda b, i: (b, i, 0))
    hpb = tm // HALO
    return pl.pallas_call(
        functools.partial(_conv_out_kernel, tm=tm, rc=rc),
        grid=(B, S // tm),
        in_specs=[tile(E),
                  pl.BlockSpec((1, HALO, E), lambda b, i: (b, jnp.maximum(i * hpb - 1, 0), 0)),
                  tile(E), tile(D),
                  full((CONV_WIDTH, E)), full((1, E)), full((1, E)), full((1, E)),
                  full((E, D)), full((1, D)), full((1, D))],
        out_specs=[tile(D), tile(D), tile(D)],
        out_shape=[jax.ShapeDtypeStruct((B, S, D), F32),
                   jax.ShapeDtypeStruct((B, S, D), BF16),
                   jax.ShapeDtypeStruct((B, S, D), BF16)],
        scratch_shapes=[pltpu.VMEM((HALO + tm, E), F32), pltpu.VMEM((tm, E), F32)],
        compiler_params=_params(("arbitrary", "arbitrary")),
    )(u, u, sz, x, conv_w, row(conv_b), row(ln_g), row(ln_b), w_out_bf16, row(g_kv), row(g_b))


def _proj_kernel(x_ref, w_ref, o_ref, w_s, *, nr, tl, epilogue, scale):
    D = w_ref.shape[0]
    first = (pl.program_id(1) == 0) & (pl.program_id(2) == 0) & (pl.program_id(3) == 0)

    @pl.when(first)
    def _():
        w_s[...] = w_ref[...].astype(BF16)

    nh = o_ref.shape[1]
    for q in range(nr):
        acc = jnp.dot(x_ref[0, :, q * D:(q + 1) * D], w_s[...], preferred_element_type=F32)
        if epilogue == "silu":
            acc = acc * jax.nn.sigmoid(acc)
        elif epilogue == "scale":
            acc = acc * scale
        for hh in range(nh):
            o_ref[0, hh, q * tl:(q + 1) * tl, :] = acc[:, hh * HEAD_DIM:(hh + 1) * HEAD_DIM].astype(o_ref.dtype)


def _proj(act, w, col_off, n_cols, dil, out_dtype, epilogue="none", scale=1.0, tn=1024, rows=1024):
    B, S, D = act.shape
    L = S // dil
    tl = min(rows, L)
    nr = rows // tl
    view = act.reshape(B, L, dil * D)
    nh = tn // HEAD_DIM
    cb0 = col_off // tn
    if L // tl > 1:
        out_row = lambda n, b, i, j: i
    else:
        out_row = lambda n, b, i, j: j
    return pl.pallas_call(
        functools.partial(_proj_kernel, nr=nr, tl=tl, epilogue=epilogue, scale=scale),
        grid=(n_cols // tn, B, L // tl, dil // nr),
        in_specs=[pl.BlockSpec((1, tl, nr * D), lambda n, b, i, j: (b, i, j)),
                  pl.BlockSpec((D, tn), lambda n, b, i, j: (0, cb0 + n))],
        out_specs=pl.BlockSpec((1, nh, rows, HEAD_DIM),
                               lambda n, b, i, j: (b, n, out_row(n, b, i, j), 0)),
        out_shape=jax.ShapeDtypeStruct((B, n_cols // HEAD_DIM, S, HEAD_DIM), out_dtype),
        scratch_shapes=[pltpu.VMEM((D, tn), BF16)],
        compiler_params=_params(("arbitrary",) * 4),
    )(view, w)


def _bucket_tables():
    delta = (jnp.arange(BLOCK)[:, None] + BLOCK) - jnp.arange(2 * BLOCK)[None, :]
    tabs = []
    for window, dil in DILATED_GROUPS:
        w_sub = window // dil
        local = (delta >= 0) & (delta <= w_sub)
        dist = jnp.clip(delta, 0) * dil
        large = MAX_EXACT + (jnp.log(jnp.maximum(dist, 1).astype(F32) / MAX_EXACT)
                             / math.log(MAX_DISTANCE / MAX_EXACT)
                             * (N_BUCKETS - MAX_EXACT)).astype(jnp.int32)
        large = jnp.minimum(large, N_BUCKETS - 1)
        bucket = jnp.where(dist < MAX_EXACT, dist, large)
        tabs.append(jnp.where(local, bucket, -1).astype(jnp.int32))
    return jnp.stack(tabs)


def _attn_kernel(rb_ref, bkt_ref, q1, q2, q3, k1, k2, k3, v1, v2, v3, sz_ref, y_ref,
                 tbl_s, o_s, l_s, *, S, mc):
    h = pl.program_id(1)
    for g in range(N_GROUPS):
        bk = bkt_ref[g]
        t = jnp.full(bk.shape, -jnp.inf, F32)
        for kk in range(N_BUCKETS):
            t = jnp.where(bk == kk, rb_ref[kk * N_HEADS + h], t)
        tbl_s[g] = t

    groups = ((q1, k1, v1), (q2, k2, v2), (q3, k3, v3))
    for g, (qr, kr, vr) in enumerate(groups):
        dil = DILATED_GROUPS[g][1]
        nb = S // dil // BLOCK

        def do_block(r, lb, has_prev, g=g, qr=qr, kr=kr, vr=vr, dil=dil, nb=nb):
            row = pl.multiple_of((r * nb + lb) * BLOCK, BLOCK)
            q = qr[0, 0, pl.ds(row, BLOCK), :]
            if has_prev:
                krow = pl.multiple_of(row - BLOCK, BLOCK)
                k = kr[0, 0, pl.ds(krow, 2 * BLOCK), :]
                v = vr[0, 0, pl.ds(krow, 2 * BLOCK), :]
                bias = tbl_s[g]
            else:
                k = kr[0, 0, pl.ds(row, BLOCK), :]
                v = vr[0, 0, pl.ds(row, BLOCK), :]
                bias = tbl_s[g, :, BLOCK:]
            s = lax.dot_general(q, k, (((1,), (1,)), ((), ())), preferred_element_type=F32) + bias
            m = jnp.max(s, axis=-1, keepdims=True)
            e = jnp.exp(s - m)
            den = jnp.sum(e, axis=-1, keepdims=True)
            acc = jnp.dot(e.astype(BF16), v, preferred_element_type=F32)
            o = acc / den
            lse = jnp.broadcast_to(m + jnp.log(den), (BLOCK, HEAD_DIM))
            t0 = lb * (BLOCK * dil) + r
            if dil == 1:
                idx = pl.ds(pl.multiple_of(t0, BLOCK), BLOCK)
            else:
                idx = pl.ds(t0, BLOCK, stride=dil)
            o_s[g, idx, :] = o
            l_s[g, idx, :] = lse

        def r_body(r, carry, do_block=do_block, nb=nb):
            do_block(r, 0, False)

            def lb_body(lb, c2):
                do_block(r, lb, True)
                return c2

            lax.fori_loop(1, nb, lb_body, 0)
            return carry

        if dil == 1:
            r_body(0, 0)
        else:
            lax.fori_loop(0, dil, r_body, 0)

    def merge(c, carry):
        rows = pl.ds(pl.multiple_of(c * mc, mc), mc)
        l1, l2, l3 = l_s[0, rows, :], l_s[1, rows, :], l_s[2, rows, :]
        lmax = jnp.maximum(jnp.maximum(l1, l2), l3)
        w1, w2, w3 = jnp.exp(l1 - lmax), jnp.exp(l2 - lmax), jnp.exp(l3 - lmax)
        o = (w1 * o_s[0, rows, :] + w2 * o_s[1, rows, :] + w3 * o_s[2, rows, :]) / (w1 + w2 + w3)
        y_ref[0, rows, :] = (o * sz_ref[0, 0, rows, :]).astype(y_ref.dtype)
        return carry

    lax.fori_loop(0, S // mc, merge, 0)


def _attention(rel_bias, qs, kvs, sz, mc=64):
    B, H, S, Dh = qs[0].shape
    bkt = _bucket_tables()
    hm = lambda off: pl.BlockSpec((1, 1, S, Dh), lambda b, h: (b, h + off, 0, 0))
    in_specs = ([pl.BlockSpec(memory_space=pltpu.SMEM),
                 pl.BlockSpec(bkt.shape, lambda b, h: (0, 0, 0))]
                + [hm(0)] * 3 + [hm(0)] * 3 + [hm(H)] * 3 + [hm(0)])
    return pl.pallas_call(
        functools.partial(_attn_kernel, S=S, mc=mc),
        grid=(B, H),
        in_specs=in_specs,
        out_specs=pl.BlockSpec((1, S, Dh), lambda b, h: (b, 0, h)),
        out_shape=jax.ShapeDtypeStruct((B, S, H * Dh), BF16),
        scratch_shapes=[pltpu.VMEM((N_GROUPS, BLOCK, 2 * BLOCK), F32),
                        pltpu.VMEM((N_GROUPS, S, Dh), F32),
                        pltpu.VMEM((N_GROUPS, S, Dh), F32)],
        compiler_params=_params(("arbitrary", "arbitrary")),
    )(rel_bias.reshape(-1), bkt, *qs, *kvs, *kvs, sz)


def _final_kernel(y_ref, w_ref, x1_ref, g_ref, o_ref):
    x2 = x1_ref[...] + jnp.dot(y_ref[...], w_ref[...], preferred_element_type=F32)
    ms = jnp.mean(x2 * x2, axis=-1, keepdims=True)
    o_ref[...] = x2 * lax.rsqrt(ms + EPS) * g_ref[...]


def _final(y, w_bf16, x1, g, tm=512):
    M, E = y.shape
    D = w_bf16.shape[1]
    return pl.pallas_call(
        _final_kernel,
        grid=(M // tm,),
        in_specs=[pl.BlockSpec((tm, E), lambda i: (i, 0)),
                  pl.BlockSpec((E, D), lambda i: (0, 0)),
                  pl.BlockSpec((tm, D), lambda i: (i, 0)),
                  pl.BlockSpec((1, D), lambda i: (0, 0))],
        out_specs=pl.BlockSpec((tm, D), lambda i: (i, 0)),
        out_shape=jax.ShapeDtypeStruct((M, D), F32),
        compiler_params=_params(("arbitrary",)),
    )(y, w_bf16, x1, g.reshape(1, D))


def kernel(x, a_norm, a_w_in, a_conv_w, a_conv_b, a_ln_g, a_ln_b, a_w_out, kv_norm, w_kv,
           b_norm, b_w_in, b_w_out, rel_bias, final_norm):
    B, S, D = x.shape
    M = B * S
    h0 = _rmsnorm_bf16(x.reshape(M, D), a_norm[0])
    u, sz0 = _inproj_a(h0, a_w_in[0])
    E = u.shape[-1]
    x1, hk, hb = _conv_out(u.reshape(B, S, E), sz0.reshape(B, S, E), x, a_conv_w[0], a_conv_b[0],
                           a_ln_g[0], a_ln_b[0], _cast_bf16(a_w_out[0]), kv_norm, b_norm[0])
    A = N_HEADS * HEAD_DIM
    qs, kvs = [], []
    for g, (_, dil) in enumerate(DILATED_GROUPS):
        qs.append(_proj(hb, b_w_in[0], g * A, A, dil, BF16, "scale", HEAD_DIM ** -0.5))
        kvs.append(_proj(hk, w_kv, 2 * g * A, 2 * A, dil, BF16))
    sz1 = _proj(hb, b_w_in[0], N_GROUPS * A, A, 1, F32, "silu")
    y = _attention(rel_bias, qs, kvs, sz1)
    out = _final(y.reshape(M, A), _cast_bf16(b_w_out[0]), x1.reshape(M, D), final_norm)
    return out.reshape(B, S, D)
```

```python
import functools
import math

import jax
import jax.numpy as jnp
from jax import lax
from jax.experimental import pallas as pl
from jax.experimental.pallas import tpu as pltpu

D_MODEL = 2048
CONV_WIDTH = 31
HEAD_DIM = 128
N_HEADS = D_MODEL // HEAD_DIM
DILATED_GROUPS = ((128, 1), (512, 4), (2048, 16))
N_GROUPS = len(DILATED_GROUPS)
BLOCK = 128
N_BUCKETS = 32
MAX_EXACT = N_BUCKETS // 2
MAX_DISTANCE = 2048
EPS = 1e-6

LANES = 128
SUBLANES = 8
HALO = 32

F32 = jnp.float32
BF16 = jnp.bfloat16
VMEM_LIMIT = 56 * 1024 * 1024


def _params(sem):
    return pltpu.CompilerParams(dimension_semantics=sem, vmem_limit_bytes=VMEM_LIMIT)


def _rmsnorm_kernel(x_ref, g_ref, o_ref):
    x = x_ref[...]
    ms = jnp.mean(x * x, axis=-1, keepdims=True)
    o_ref[...] = (x * lax.rsqrt(ms + EPS) * g_ref[...]).astype(o_ref.dtype)


def _rmsnorm_bf16(x2d, g, tm=512):
    M, D = x2d.shape
    return pl.pallas_call(
        _rmsnorm_kernel,
        grid=(M // tm,),
        in_specs=[pl.BlockSpec((tm, D), lambda i: (i, 0)),
                  pl.BlockSpec((1, D), lambda i: (0, 0))],
        out_specs=pl.BlockSpec((tm, D), lambda i: (i, 0)),
        out_shape=jax.ShapeDtypeStruct((M, D), BF16),
        compiler_params=_params(("arbitrary",)),
        name="prenorm",
    )(x2d, g.reshape(1, D))


def _cast_kernel(w_ref, o_ref):
    o_ref[...] = w_ref[...].astype(o_ref.dtype)


def _cast_bf16(w, tr=512):
    R, C = w.shape
    return pl.pallas_call(
        _cast_kernel,
        grid=(R // tr,),
        in_specs=[pl.BlockSpec((tr, C), lambda i: (i, 0))],
        out_specs=pl.BlockSpec((tr, C), lambda i: (i, 0)),
        out_shape=jax.ShapeDtypeStruct((R, C), BF16),
        compiler_params=_params(("arbitrary",)),
        name="cast_bf16",
    )(w)


def _inproj_a_kernel(h_ref, wa_ref, wb_ref, wz_ref, u_ref, sz_ref, wa_s, wb_s, wz_s):
    @pl.when(pl.program_id(1) == 0)
    def _():
        wa_s[...] = wa_ref[...].astype(BF16)
        wb_s[...] = wb_ref[...].astype(BF16)
        wz_s[...] = wz_ref[...].astype(BF16)

    h = h_ref[...]
    a = jnp.dot(h, wa_s[...], preferred_element_type=F32)
    b = jnp.dot(h, wb_s[...], preferred_element_type=F32)
    u_ref[...] = a * jax.nn.sigmoid(b)
    z = jnp.dot(h, wz_s[...], preferred_element_type=F32)
    sz_ref[...] = z * jax.nn.sigmoid(z)


def _inproj_a(h, w_in, tm=1024, tn=512):
    M, D = h.shape
    E = w_in.shape[1] // 3
    nb = E // tn
    return pl.pallas_call(
        _inproj_a_kernel,
        grid=(nb, M // tm),
        in_specs=[pl.BlockSpec((tm, D), lambda n, m: (m, 0)),
                  pl.BlockSpec((D, tn), lambda n, m: (0, n)),
                  pl.BlockSpec((D, tn), lambda n, m: (0, n + nb)),
                  pl.BlockSpec((D, tn), lambda n, m: (0, n + 2 * nb))],
        out_specs=[pl.BlockSpec((tm, tn), lambda n, m: (m, n)),
                   pl.BlockSpec((tm, tn), lambda n, m: (m, n))],
        out_shape=[jax.ShapeDtypeStruct((M, E), F32),
                   jax.ShapeDtypeStruct((M, E), F32)],
        scratch_shapes=[pltpu.VMEM((D, tn), BF16)] * 3,
        compiler_params=_params(("arbitrary", "arbitrary")),
        name="inproj_a",
    )(h, w_in, w_in, w_in)


def _conv_out_kernel(ucur_ref, uprev_ref, sz_ref, x_ref, cw_ref, cb_ref, lg_ref, lb_ref,
                     wo_ref, gkv_ref, gb_ref, x1_ref, hk_ref, hb_ref, seg_s, c_s, *, tm, pitch, oc):
    E = ucur_ref.shape[-1]
    A = tm // SUBLANES
    first = pl.program_id(1) == 0
    base = HALO - (CONV_WIDTH - 1)
    for ct in range(E // LANES):
        cs = slice(ct * LANES, (ct + 1) * LANES)
        seg_s[ct, 0:HALO, :] = jnp.where(first, 0.0, uprev_ref[0, :, cs])
        seg_s[ct, HALO:HALO + A, :] = ucur_ref[0, 0:A, cs]
        for i in range(1, SUBLANES):
            seg_s[ct, i * pitch:i * pitch + HALO + A, :] = ucur_ref[0, i * A - HALO:(i + 1) * A, cs]
        wv = [cw_ref[ct, pl.ds(j, SUBLANES, stride=0), :] for j in range(CONV_WIDTH)]
        bias = cb_ref[ct, pl.ds(0, SUBLANES, stride=0), :]
        for a0 in range(0, A, oc):
            accs = [None] * oc
            for k in range(oc + CONV_WIDTH - 1):
                w = seg_s[ct, pl.ds(a0 + base + k, SUBLANES, stride=pitch), :]
                for o in range(oc):
                    j = k - o
                    if 0 <= j < CONV_WIDTH:
                        t = w * wv[j]
                        accs[o] = t if accs[o] is None else accs[o] + t
            for o in range(oc):
                c_s[ct, pl.ds(a0 + o, SUBLANES, stride=A), :] = accs[o] + bias

    c = jnp.concatenate([c_s[ct] for ct in range(E // LANES)], axis=-1)
    mu = jnp.mean(c, axis=-1, keepdims=True)
    cc = c - mu
    var = jnp.mean(cc * cc, axis=-1, keepdims=True)
    cn = cc * lax.rsqrt(var + EPS) * lg_ref[...] + lb_ref[...]
    y = (cn * jax.nn.sigmoid(cn)) * sz_ref[0]
    o = jnp.dot(y.astype(BF16), wo_ref[...], preferred_element_type=F32)
    x1 = x_ref[0] + o
    x1_ref[0] = x1
    xn = x1 * lax.rsqrt(jnp.mean(x1 * x1, axis=-1, keepdims=True) + EPS)
    hk_ref[0] = (xn * gkv_ref[...]).astype(BF16)
    hb_ref[0] = (xn * gb_ref[...]).astype(BF16)


def _conv_out(u, sz, x, conv_w, conv_b, ln_g, ln_b, w_out_bf16, g_kv, g_b, tm=256, oc=8):
    B, S, E = u.shape
    D = x.shape[-1]
    nct = E // LANES
    pitch = HALO + tm // SUBLANES + SUBLANES
    row = lambda a: a.reshape(1, -1)
    full = lambda shape: pl.BlockSpec(shape, lambda b, i: (0,) * len(shape))
    tile = lambda w: pl.BlockSpec((1, tm, w), lambda b, i: (b, i, 0))
    hpb = tm // HALO
    return pl.pallas_call(
        functools.partial(_conv_out_kernel, tm=tm, pitch=pitch, oc=oc),
        grid=(B, S // tm),
        in_specs=[tile(E),
                  pl.BlockSpec((1, HALO, E), lambda b, i: (b, jnp.maximum(i * hpb - 1, 0), 0)),
                  tile(E), tile(D),
                  full((nct, CONV_WIDTH, LANES)), full((nct, 1, LANES)), full((1, E)), full((1, E)),
                  full((E, D)), full((1, D)), full((1, D))],
        out_specs=[tile(D), tile(D), tile(D)],
        out_shape=[jax.ShapeDtypeStruct((B, S, D), F32),
                   jax.ShapeDtypeStruct((B, S, D), BF16),
                   jax.ShapeDtypeStruct((B, S, D), BF16)],
        scratch_shapes=[pltpu.VMEM((nct, SUBLANES * pitch, LANES), F32),
                        pltpu.VMEM((nct, tm, LANES), F32)],
        compiler_params=_params(("arbitrary", "arbitrary")),
        name="conv_out",
    )(u, u, sz, x, conv_w.reshape(CONV_WIDTH, nct, LANES).transpose(1, 0, 2),
      conv_b.reshape(nct, 1, LANES), row(ln_g), row(ln_b), w_out_bf16, row(g_kv), row(g_b))


def _proj_kernel(x_ref, w_ref, o_ref, w_s, acc_s, *, dil, epilogue, scale):
    first = (pl.program_id(1) == 0) & (pl.program_id(2) == 0)

    @pl.when(first)
    def _():
        w_s[...] = w_ref[...].astype(BF16)

    acc = jnp.dot(x_ref[0], w_s[...], preferred_element_type=F32)
    if epilogue == "silu":
        acc = acc * jax.nn.sigmoid(acc)
    elif epilogue == "scale":
        acc = acc * scale
    nh = o_ref.shape[1]
    tl = o_ref.shape[3]
    if dil == 1:
        for hh in range(nh):
            o_ref[0, hh, 0] = acc[:, hh * HEAD_DIM:(hh + 1) * HEAD_DIM].astype(o_ref.dtype)
    else:
        for hh in range(nh):
            acc_s[hh] = acc[:, hh * HEAD_DIM:(hh + 1) * HEAD_DIM]
        for hh in range(nh):
            for r in range(dil):
                o_ref[0, hh, r] = acc_s[hh, pl.ds(r, tl, stride=dil), :].astype(o_ref.dtype)


def _proj(act, w, col_off, n_cols, dil, out_dtype, epilogue="none", scale=1.0, tn=1024, tm=1024):
    B, S, D = act.shape
    L = S // dil
    tl = tm // dil
    nh = tn // HEAD_DIM
    cb0 = col_off // tn
    out = pl.pallas_call(
        functools.partial(_proj_kernel, dil=dil, epilogue=epilogue, scale=scale),
        grid=(n_cols // tn, B, S // tm),
        in_specs=[pl.BlockSpec((1, tm, D), lambda n, b, i: (b, i, 0)),
                  pl.BlockSpec((D, tn), lambda n, b, i: (0, cb0 + n))],
        out_specs=pl.BlockSpec((1, nh, dil, tl, HEAD_DIM), lambda n, b, i: (b, n, 0, i, 0)),
        out_shape=jax.ShapeDtypeStruct((B, n_cols // HEAD_DIM, dil, L, HEAD_DIM), out_dtype),
        scratch_shapes=[pltpu.VMEM((D, tn), BF16), pltpu.VMEM((nh, tm, HEAD_DIM), F32)],
        compiler_params=_params(("arbitrary",) * 3),
        name="proj_d%d_%s" % (dil, epilogue),
    )(act, w)
    return out.reshape(B, n_cols // HEAD_DIM, S, HEAD_DIM)


def _bucket_tables():
    delta = (jnp.arange(BLOCK)[:, None] + BLOCK) - jnp.arange(2 * BLOCK)[None, :]
    tabs = []
    for window, dil in DILATED_GROUPS:
        w_sub = window // dil
        local = (delta >= 0) & (delta <= w_sub)
        dist = jnp.clip(delta, 0) * dil
        large = MAX_EXACT + (jnp.log(jnp.maximum(dist, 1).astype(F32) / MAX_EXACT)
                             / math.log(MAX_DISTANCE / MAX_EXACT)
                             * (N_BUCKETS - MAX_EXACT)).astype(jnp.int32)
        large = jnp.minimum(large, N_BUCKETS - 1)
        bucket = jnp.where(dist < MAX_EXACT, dist, large)
        tabs.append(jnp.where(local, bucket, -1).astype(jnp.int32))
    return jnp.stack(tabs)


def _attn_kernel(rb_ref, bkt_ref, q1, q2, q3, k1, k2, k3, v1, v2, v3, sz_ref, y_ref,
                 tbl_s, o_s, m_s, d_s, *, S, mc, unroll, tile):
    h = pl.program_id(0)

    @pl.when(pl.program_id(1) == 0)
    def _():
        for g in range(N_GROUPS):
            bk = bkt_ref[g]
            t = jnp.full(bk.shape, -jnp.inf, F32)
            for kk in range(N_BUCKETS):
                t = jnp.where(bk == kk, rb_ref[kk * N_HEADS + h], t)
            tbl_s[2 * g] = t
            tbl_s[2 * g + 1, :, 0:BLOCK] = t[:, BLOCK:]
            tbl_s[2 * g + 1, :, BLOCK:] = jnp.full((BLOCK, BLOCK), -jnp.inf, F32)

    groups = ((q1, k1, v1), (q2, k2, v2), (q3, k3, v3))
    bpt = tile // BLOCK

    def tile_body(T, carry):
        for g, (qr, kr, vr) in enumerate(groups):
            dil = DILATED_GROUPS[g][1]
            nb = S // BLOCK // dil
            bpr = bpt // dil
            sh = bpr.bit_length() - 1

            def do_block(j, g=g, qr=qr, kr=kr, vr=vr, dil=dil, nb=nb, bpr=bpr, sh=sh):
                r = lax.shift_right_logical(j, sh)
                jl = lax.bitwise_and(j, bpr - 1)
                lb = T * bpr + jl
                p = r * nb + lb
                is_first = (lb == 0).astype(jnp.int32)
                row = pl.multiple_of(p * BLOCK, BLOCK)
                krow = pl.multiple_of((p - 1 + is_first) * BLOCK, BLOCK)
                q = qr[0, 0, pl.ds(row, BLOCK), :]
                k = kr[0, 0, pl.ds(krow, 2 * BLOCK), :]
                v = vr[0, 0, pl.ds(krow, 2 * BLOCK), :]
                s = lax.dot_general(q, k, (((1,), (1,)), ((), ())), preferred_element_type=F32)
                s = s + tbl_s[2 * g + is_first]
                m = jnp.max(s, axis=-1, keepdims=True)
                e = jnp.exp(s - m)
                den = jnp.sum(e, axis=-1, keepdims=True)
                acc = jnp.dot(e.astype(BF16), v, preferred_element_type=F32)
                t0 = jl * (BLOCK * dil) + r
                if dil == 1:
                    idx = pl.ds(pl.multiple_of(t0, BLOCK), BLOCK)
                else:
                    idx = pl.ds(t0, BLOCK, stride=dil)
                o_s[g, idx, :] = acc
                m_s[g, idx, :] = jnp.broadcast_to(m, (BLOCK, HEAD_DIM))
                d_s[g, idx, :] = jnp.broadcast_to(den, (BLOCK, HEAD_DIM))

            def body(it, c2, do_block=do_block):
                for uu in range(unroll):
                    do_block(it * unroll + uu)
                return c2

            lax.fori_loop(0, bpt // unroll, body, 0)

        def merge(c, c2):
            rows = pl.ds(pl.multiple_of(c * mc, mc), mc)
            m1, m2, m3 = m_s[0, rows, :], m_s[1, rows, :], m_s[2, rows, :]
            mx = jnp.maximum(jnp.maximum(m1, m2), m3)
            w1, w2, w3 = jnp.exp(m1 - mx), jnp.exp(m2 - mx), jnp.exp(m3 - mx)
            num = w1 * o_s[0, rows, :] + w2 * o_s[1, rows, :] + w3 * o_s[2, rows, :]
            den = w1 * d_s[0, rows, :] + w2 * d_s[1, rows, :] + w3 * d_s[2, rows, :]
            orow = pl.ds(pl.multiple_of(T * tile + c * mc, mc), mc)
            y_ref[0, orow, :] = ((num / den) * sz_ref[0, 0, orow, :]).astype(y_ref.dtype)
            return c2

        lax.fori_loop(0, tile // mc, merge, 0)
        return carry

    lax.fori_loop(0, S // tile, tile_body, 0)


def _attention(rel_bias, qs, kvs, sz, mc=64, unroll=8, tile=2048):
    B, H, S, Dh = qs[0].shape
    bkt = _bucket_tables()
    hm = lambda off: pl.BlockSpec((1, 1, S, Dh), lambda h, b: (b, h + off, 0, 0))
    in_specs = ([pl.BlockSpec(memory_space=pltpu.SMEM),
                 pl.BlockSpec(bkt.shape, lambda h, b: (0, 0, 0))]
                + [hm(0)] * 3 + [hm(0)] * 3 + [hm(H)] * 3 + [hm(0)])
    return pl.pallas_call(
        functools.partial(_attn_kernel, S=S, mc=mc, unroll=unroll, tile=tile),
        grid=(H, B),
        in_specs=in_specs,
        out_specs=pl.BlockSpec((1, S, Dh), lambda h, b: (b, 0, h)),
        out_shape=jax.ShapeDtypeStruct((B, S, H * Dh), BF16),
        scratch_shapes=[pltpu.VMEM((2 * N_GROUPS, BLOCK, 2 * BLOCK), F32),
                        pltpu.VMEM((N_GROUPS, tile, Dh), F32),
                        pltpu.VMEM((N_GROUPS, tile, Dh), F32),
                        pltpu.VMEM((N_GROUPS, tile, Dh), F32)],
        compiler_params=_params(("arbitrary", "arbitrary")),
        name="attention",
    )(rel_bias.reshape(-1), bkt, *qs, *kvs, *kvs, sz)


def _final_kernel(y_ref, w_ref, x1_ref, g_ref, o_ref):
    x2 = x1_ref[...] + jnp.dot(y_ref[...], w_ref[...], preferred_element_type=F32)
    ms = jnp.mean(x2 * x2, axis=-1, keepdims=True)
    o_ref[...] = x2 * lax.rsqrt(ms + EPS) * g_ref[...]


def _final(y, w_bf16, x1, g, tm=512):
    M, E = y.shape
    D = w_bf16.shape[1]
    return pl.pallas_call(
        _final_kernel,
        grid=(M // tm,),
        in_specs=[pl.BlockSpec((tm, E), lambda i: (i, 0)),
                  pl.BlockSpec((E, D), lambda i: (0, 0)),
                  pl.BlockSpec((tm, D), lambda i: (i, 0)),
                  pl.BlockSpec((1, D), lambda i: (0, 0))],
        out_specs=pl.BlockSpec((tm, D), lambda i: (i, 0)),
        out_shape=jax.ShapeDtypeStruct((M, D), F32),
        compiler_params=_params(("arbitrary",)),
        name="outproj_final",
    )(y, w_bf16, x1, g.reshape(1, D))


def kernel(x, a_norm, a_w_in, a_conv_w, a_conv_b, a_ln_g, a_ln_b, a_w_out, kv_norm, w_kv,
           b_norm, b_w_in, b_w_out, rel_bias, final_norm):
    B, S, D = x.shape
    M = B * S
    h0 = _rmsnorm_bf16(x.reshape(M, D), a_norm[0])
    u, sz0 = _inproj_a(h0, a_w_in[0])
    E = u.shape[-1]
    x1, hk, hb = _conv_out(u.reshape(B, S, E), sz0.reshape(B, S, E), x, a_conv_w[0], a_conv_b[0],
                           a_ln_g[0], a_ln_b[0], _cast_bf16(a_w_out[0]), kv_norm, b_norm[0])
    A = N_HEADS * HEAD_DIM
    qs, kvs = [], []
    for g, (_, dil) in enumerate(DILATED_GROUPS):
        qs.append(_proj(hb, b_w_in[0], g * A, A, dil, BF16, "scale", HEAD_DIM ** -0.5))
        kvs.append(_proj(hk, w_kv, 2 * g * A, 2 * A, dil, BF16))
    sz1 = _proj(hb, b_w_in[0], N_GROUPS * A, A, 1, F32, "silu")
    y = _attention(rel_bias, qs, kvs, sz1)
    out = _final(y.reshape(M, A), _cast_bf16(b_w_out[0]), x1.reshape(M, D), final_norm)
    return out.reshape(B, S, D)
```

```python
import functools
import math

import jax
import jax.numpy as jnp
from jax import lax
from jax.experimental import pallas as pl
from jax.experimental.pallas import tpu as pltpu

D_MODEL = 2048
CONV_WIDTH = 31
HEAD_DIM = 128
N_HEADS = D_MODEL // HEAD_DIM
DILATED_GROUPS = ((128, 1), (512, 4), (2048, 16))
N_GROUPS = len(DILATED_GROUPS)
BLOCK = 128
N_BUCKETS = 32
MAX_EXACT = N_BUCKETS // 2
MAX_DISTANCE = 2048
EPS = 1e-6

LANES = 128
SUBLANES = 8
HALO = 32

F32 = jnp.float32
BF16 = jnp.bfloat16
VMEM_LIMIT = 56 * 1024 * 1024


def _params(sem):
    return pltpu.CompilerParams(dimension_semantics=sem, vmem_limit_bytes=VMEM_LIMIT)


def _rmsnorm_kernel(x_ref, g_ref, o_ref):
    x = x_ref[...]
    ms = jnp.mean(x * x, axis=-1, keepdims=True)
    o_ref[...] = (x * lax.rsqrt(ms + EPS) * g_ref[...]).astype(o_ref.dtype)


def _rmsnorm_bf16(x2d, g, tm=512):
    M, D = x2d.shape
    return pl.pallas_call(
        _rmsnorm_kernel,
        grid=(M // tm,),
        in_specs=[pl.BlockSpec((tm, D), lambda i: (i, 0)),
                  pl.BlockSpec((1, D), lambda i: (0, 0))],
        out_specs=pl.BlockSpec((tm, D), lambda i: (i, 0)),
        out_shape=jax.ShapeDtypeStruct((M, D), BF16),
        compiler_params=_params(("arbitrary",)),
        name="prenorm",
    )(x2d, g.reshape(1, D))


def _cast_kernel(w_ref, o_ref):
    o_ref[...] = w_ref[...].astype(o_ref.dtype)


def _cast_bf16(w, tr=512):
    R, C = w.shape
    return pl.pallas_call(
        _cast_kernel,
        grid=(R // tr,),
        in_specs=[pl.BlockSpec((tr, C), lambda i: (i, 0))],
        out_specs=pl.BlockSpec((tr, C), lambda i: (i, 0)),
        out_shape=jax.ShapeDtypeStruct((R, C), BF16),
        compiler_params=_params(("arbitrary",)),
        name="cast_bf16",
    )(w)


def _inproj_a_kernel(h_ref, wa_ref, wb_ref, wz_ref, u_ref, sz_ref, wa_s, wb_s, wz_s):
    @pl.when(pl.program_id(1) == 0)
    def _():
        wa_s[...] = wa_ref[...].astype(BF16)
        wb_s[...] = wb_ref[...].astype(BF16)
        wz_s[...] = wz_ref[...].astype(BF16)

    h = h_ref[...]
    a = jnp.dot(h, wa_s[...], preferred_element_type=F32)
    b = jnp.dot(h, wb_s[...], preferred_element_type=F32)
    u_ref[...] = a * jax.nn.sigmoid(b)
    z = jnp.dot(h, wz_s[...], preferred_element_type=F32)
    sz_ref[...] = z * jax.nn.sigmoid(z)


def _inproj_a(h, w_in, tm=1024, tn=512):
    M, D = h.shape
    E = w_in.shape[1] // 3
    nb = E // tn
    return pl.pallas_call(
        _inproj_a_kernel,
        grid=(nb, M // tm),
        in_specs=[pl.BlockSpec((tm, D), lambda n, m: (m, 0)),
                  pl.BlockSpec((D, tn), lambda n, m: (0, n)),
                  pl.BlockSpec((D, tn), lambda n, m: (0, n + nb)),
                  pl.BlockSpec((D, tn), lambda n, m: (0, n + 2 * nb))],
        out_specs=[pl.BlockSpec((tm, tn), lambda n, m: (m, n)),
                   pl.BlockSpec((tm, tn), lambda n, m: (m, n))],
        out_shape=[jax.ShapeDtypeStruct((M, E), F32),
                   jax.ShapeDtypeStruct((M, E), F32)],
        scratch_shapes=[pltpu.VMEM((D, tn), BF16)] * 3,
        compiler_params=_params(("arbitrary", "arbitrary")),
        name="inproj_a",
    )(h, w_in, w_in, w_in)


def _conv_out_kernel(ucur_ref, uprev_ref, sz_ref, x_ref, cw_ref, cb_ref, lg_ref, lb_ref,
                     wo_ref, gkv_ref, gb_ref, x1_ref, hk_ref, hb_ref, seg_s, c_s, *, tm, pitch, oc):
    E = ucur_ref.shape[-1]
    A = tm // SUBLANES
    first = pl.program_id(1) == 0
    base = HALO - (CONV_WIDTH - 1)
    for ct in range(E // LANES):
        cs = slice(ct * LANES, (ct + 1) * LANES)
        seg_s[ct, 0:HALO, :] = jnp.where(first, 0.0, uprev_ref[0, :, cs])
        seg_s[ct, HALO:HALO + A, :] = ucur_ref[0, 0:A, cs]
        for i in range(1, SUBLANES):
            seg_s[ct, i * pitch:i * pitch + HALO + A, :] = ucur_ref[0, i * A - HALO:(i + 1) * A, cs]
        wv = [cw_ref[ct, pl.ds(j, SUBLANES, stride=0), :] for j in range(CONV_WIDTH)]
        bias = cb_ref[ct, pl.ds(0, SUBLANES, stride=0), :]
        for a0 in range(0, A, oc):
            accs = [None] * oc
            for k in range(oc + CONV_WIDTH - 1):
                w = seg_s[ct, pl.ds(a0 + base + k, SUBLANES, stride=pitch), :]
                for o in range(oc):
                    j = k - o
                    if 0 <= j < CONV_WIDTH:
                        t = w * wv[j]
                        accs[o] = t if accs[o] is None else accs[o] + t
            for o in range(oc):
                c_s[ct, pl.ds(a0 + o, SUBLANES, stride=A), :] = accs[o] + bias

    c = jnp.concatenate([c_s[ct] for ct in range(E // LANES)], axis=-1)
    mu = jnp.mean(c, axis=-1, keepdims=True)
    cc = c - mu
    var = jnp.mean(cc * cc, axis=-1, keepdims=True)
    cn = cc * lax.rsqrt(var + EPS) * lg_ref[...] + lb_ref[...]
    y = (cn * jax.nn.sigmoid(cn)) * sz_ref[0]
    o = jnp.dot(y.astype(BF16), wo_ref[...], preferred_element_type=F32)
    x1 = x_ref[0] + o
    x1_ref[0] = x1
    xn = x1 * lax.rsqrt(jnp.mean(x1 * x1, axis=-1, keepdims=True) + EPS)
    hk_ref[0] = (xn * gkv_ref[...]).astype(BF16)
    hb_ref[0] = (xn * gb_ref[...]).astype(BF16)


def _conv_out(u, sz, x, conv_w, conv_b, ln_g, ln_b, w_out_bf16, g_kv, g_b, tm=256, oc=8):
    B, S, E = u.shape
    D = x.shape[-1]
    nct = E // LANES
    pitch = HALO + tm // SUBLANES + SUBLANES
    row = lambda a: a.reshape(1, -1)
    full = lambda shape: pl.BlockSpec(shape, lambda b, i: (0,) * len(shape))
    tile = lambda w: pl.BlockSpec((1, tm, w), lambda b, i: (b, i, 0))
    hpb = tm // HALO
    return pl.pallas_call(
        functools.partial(_conv_out_kernel, tm=tm, pitch=pitch, oc=oc),
        grid=(B, S // tm),
        in_specs=[tile(E),
                  pl.BlockSpec((1, HALO, E), lambda b, i: (b, jnp.maximum(i * hpb - 1, 0), 0)),
                  tile(E), tile(D),
                  full((nct, CONV_WIDTH, LANES)), full((nct, 1, LANES)), full((1, E)), full((1, E)),
                  full((E, D)), full((1, D)), full((1, D))],
        out_specs=[tile(D), tile(D), tile(D)],
        out_shape=[jax.ShapeDtypeStruct((B, S, D), F32),
                   jax.ShapeDtypeStruct((B, S, D), BF16),
                   jax.ShapeDtypeStruct((B, S, D), BF16)],
        scratch_shapes=[pltpu.VMEM((nct, SUBLANES * pitch, LANES), F32),
                        pltpu.VMEM((nct, tm, LANES), F32)],
        compiler_params=_params(("arbitrary", "arbitrary")),
        name="conv_out",
    )(u, u, sz, x, conv_w.reshape(CONV_WIDTH, nct, LANES).transpose(1, 0, 2),
      conv_b.reshape(nct, 1, LANES), row(ln_g), row(ln_b), w_out_bf16, row(g_kv), row(g_b))


def _proj_kernel(x_ref, w_ref, o_ref, w_s, acc_s, *, dil, epilogue, scale):
    first = (pl.program_id(1) == 0) & (pl.program_id(2) == 0)

    @pl.when(first)
    def _():
        w_s[...] = w_ref[...].astype(BF16)

    acc = jnp.dot(x_ref[0], w_s[...], preferred_element_type=F32)
    if epilogue == "silu":
        acc = acc * jax.nn.sigmoid(acc)
    elif epilogue == "scale":
        acc = acc * scale
    nh = o_ref.shape[1]
    tl = o_ref.shape[3]
    if dil == 1:
        for hh in range(nh):
            o_ref[0, hh, 0] = acc[:, hh * HEAD_DIM:(hh + 1) * HEAD_DIM].astype(o_ref.dtype)
    else:
        for hh in range(nh):
            acc_s[hh] = acc[:, hh * HEAD_DIM:(hh + 1) * HEAD_DIM]
        for hh in range(nh):
            for r in range(dil):
                o_ref[0, hh, r] = acc_s[hh, pl.ds(r, tl, stride=dil), :].astype(o_ref.dtype)


def _proj(act, w, col_off, n_cols, dil, out_dtype, epilogue="none", scale=1.0, tn=1024, tm=1024):
    B, S, D = act.shape
    L = S // dil
    tl = tm // dil
    nh = tn // HEAD_DIM
    cb0 = col_off // tn
    out = pl.pallas_call(
        functools.partial(_proj_kernel, dil=dil, epilogue=epilogue, scale=scale),
        grid=(n_cols // tn, B, S // tm),
        in_specs=[pl.BlockSpec((1, tm, D), lambda n, b, i: (b, i, 0)),
                  pl.BlockSpec((D, tn), lambda n, b, i: (0, cb0 + n))],
        out_specs=pl.BlockSpec((1, nh, dil, tl, HEAD_DIM), lambda n, b, i: (b, n, 0, i, 0)),
        out_shape=jax.ShapeDtypeStruct((B, n_cols // HEAD_DIM, dil, L, HEAD_DIM), out_dtype),
        scratch_shapes=[pltpu.VMEM((D, tn), BF16), pltpu.VMEM((nh, tm, HEAD_DIM), F32)],
        compiler_params=_params(("arbitrary",) * 3),
        name="proj_d%d_%s" % (dil, epilogue),
    )(act, w)
    return out.reshape(B, n_cols // HEAD_DIM, S, HEAD_DIM)


def _bucket_tables():
    delta = (jnp.arange(BLOCK)[:, None] + BLOCK) - jnp.arange(2 * BLOCK)[None, :]
    tabs = []
    for window, dil in DILATED_GROUPS:
        w_sub = window // dil
        local = (delta >= 0) & (delta <= w_sub)
        dist = jnp.clip(delta, 0) * dil
        large = MAX_EXACT + (jnp.log(jnp.maximum(dist, 1).astype(F32) / MAX_EXACT)
                             / math.log(MAX_DISTANCE / MAX_EXACT)
                             * (N_BUCKETS - MAX_EXACT)).astype(jnp.int32)
        large = jnp.minimum(large, N_BUCKETS - 1)
        bucket = jnp.where(dist < MAX_EXACT, dist, large)
        tabs.append(jnp.where(local, bucket, -1).astype(jnp.int32))
    return jnp.stack(tabs)


def _attn_kernel(rb_ref, bkt_ref, q1, q2, q3, k1, k2, k3, v1, v2, v3, sz_ref, y_ref,
                 tbl_s, o_s, m_s, d_s, e_s, *, S, mc, unroll, tile):
    h = pl.program_id(0)

    @pl.when(pl.program_id(1) == 0)
    def _():
        for g in range(N_GROUPS):
            bk = bkt_ref[g]
            t = jnp.full(bk.shape, -jnp.inf, F32)
            for kk in range(N_BUCKETS):
                t = jnp.where(bk == kk, rb_ref[kk * N_HEADS + h], t)
            tbl_s[2 * g] = t
            tbl_s[2 * g + 1, :, 0:BLOCK] = t[:, BLOCK:]
            tbl_s[2 * g + 1, :, BLOCK:] = jnp.full((BLOCK, BLOCK), -jnp.inf, F32)

    groups = ((q1, k1, v1), (q2, k2, v2), (q3, k3, v3))
    bpt = tile // BLOCK

    blocks = [(g, j) for g in range(N_GROUPS) for j in range(bpt)]
    n_sets = len(blocks) // unroll

    def tile_body(T, carry):
        def geometry(g, j):
            dil = DILATED_GROUPS[g][1]
            nb = S // BLOCK // dil
            bpr = bpt // dil
            r, jl = j // bpr, j % bpr
            lb = T * bpr + jl
            p = r * nb + lb
            is_first = (lb == 0).astype(jnp.int32)
            row = pl.multiple_of(p * BLOCK, BLOCK)
            krow = pl.multiple_of((p - 1 + is_first) * BLOCK, BLOCK)
            t0 = jl * (BLOCK * dil) + r
            idx = pl.ds(t0, BLOCK) if dil == 1 else pl.ds(t0, BLOCK, stride=dil)
            return row, krow, is_first, idx

        def scores(g, j, slot):
            row, krow, is_first, idx = geometry(g, j)
            q = groups[g][0][0, 0, pl.ds(row, BLOCK), :]
            k = groups[g][1][0, 0, pl.ds(krow, 2 * BLOCK), :]
            s = lax.dot_general(q, k, (((1,), (1,)), ((), ())), preferred_element_type=F32)
            s = s + tbl_s[2 * g + is_first]
            m = jnp.max(s, axis=-1, keepdims=True)
            e_s[slot] = jnp.exp(s - m).astype(BF16)
            m_s[g, idx, :] = jnp.broadcast_to(m, (BLOCK, HEAD_DIM))

        def values(g, j, slot):
            row, krow, is_first, idx = geometry(g, j)
            v = groups[g][2][0, 0, pl.ds(krow, 2 * BLOCK), :]
            va = jnp.concatenate([v, jnp.ones_like(v)], axis=1)
            acc = jnp.dot(e_s[slot], va, preferred_element_type=F32)
            o_s[g, idx, :] = acc[:, :HEAD_DIM]
            d_s[g, idx, :] = acc[:, HEAD_DIM:]

        for k in range(n_sets + 1):
            for uu in range(unroll):
                if k < n_sets:
                    scores(*blocks[k * unroll + uu], (k % 2) * unroll + uu)
                if k > 0:
                    values(*blocks[(k - 1) * unroll + uu], ((k - 1) % 2) * unroll + uu)

        def merge(c, c2):
            rows = pl.ds(pl.multiple_of(c * mc, mc), mc)
            m1, m2, m3 = m_s[0, rows, :], m_s[1, rows, :], m_s[2, rows, :]
            mx = jnp.maximum(jnp.maximum(m1, m2), m3)
            w1, w2, w3 = jnp.exp(m1 - mx), jnp.exp(m2 - mx), jnp.exp(m3 - mx)
            num = w1 * o_s[0, rows, :] + w2 * o_s[1, rows, :] + w3 * o_s[2, rows, :]
            den = w1 * d_s[0, rows, :] + w2 * d_s[1, rows, :] + w3 * d_s[2, rows, :]
            orow = pl.ds(pl.multiple_of(T * tile + c * mc, mc), mc)
            y_ref[0, orow, :] = ((num / den) * sz_ref[0, 0, orow, :]).astype(y_ref.dtype)
            return c2

        lax.fori_loop(0, tile // mc, merge, 0)
        return carry

    lax.fori_loop(0, S // tile, tile_body, 0)


def _attention(rel_bias, qs, kvs, sz, mc=64, unroll=4, tile=2048):
    B, H, S, Dh = qs[0].shape
    bkt = _bucket_tables()
    hm = lambda off: pl.BlockSpec((1, 1, S, Dh), lambda h, b: (b, h + off, 0, 0))
    in_specs = ([pl.BlockSpec(memory_space=pltpu.SMEM),
                 pl.BlockSpec(bkt.shape, lambda h, b: (0, 0, 0))]
                + [hm(0)] * 3 + [hm(0)] * 3 + [hm(H)] * 3 + [hm(0)])
    return pl.pallas_call(
        functools.partial(_attn_kernel, S=S, mc=mc, unroll=unroll, tile=tile),
        grid=(H, B),
        in_specs=in_specs,
        out_specs=pl.BlockSpec((1, S, Dh), lambda h, b: (b, 0, h)),
        out_shape=jax.ShapeDtypeStruct((B, S, H * Dh), BF16),
        scratch_shapes=[pltpu.VMEM((2 * N_GROUPS, BLOCK, 2 * BLOCK), F32),
                        pltpu.VMEM((N_GROUPS, tile, Dh), F32),
                        pltpu.VMEM((N_GROUPS, tile, Dh), F32),
                        pltpu.VMEM((N_GROUPS, tile, Dh), F32),
                        pltpu.VMEM((2 * unroll, BLOCK, 2 * BLOCK), BF16)],
        compiler_params=_params(("arbitrary", "arbitrary")),
        name="attention",
    )(rel_bias.reshape(-1), bkt, *qs, *kvs, *kvs, sz)


def _final_kernel(y_ref, w_ref, x1_ref, g_ref, o_ref):
    x2 = x1_ref[...] + jnp.dot(y_ref[...], w_ref[...], preferred_element_type=F32)
    ms = jnp.mean(x2 * x2, axis=-1, keepdims=True)
    o_ref[...] = x2 * lax.rsqrt(ms + EPS) * g_ref[...]


def _final(y, w_bf16, x1, g, tm=512):
    M, E = y.shape
    D = w_bf16.shape[1]
    return pl.pallas_call(
        _final_kernel,
        grid=(M // tm,),
        in_specs=[pl.BlockSpec((tm, E), lambda i: (i, 0)),
                  pl.BlockSpec((E, D), lambda i: (0, 0)),
                  pl.BlockSpec((tm, D), lambda i: (i, 0)),
                  pl.BlockSpec((1, D), lambda i: (0, 0))],
        out_specs=pl.BlockSpec((tm, D), lambda i: (i, 0)),
        out_shape=jax.ShapeDtypeStruct((M, D), F32),
        compiler_params=_params(("arbitrary",)),
        name="outproj_final",
    )(y, w_bf16, x1, g.reshape(1, D))


def kernel(x, a_norm, a_w_in, a_conv_w, a_conv_b, a_ln_g, a_ln_b, a_w_out, kv_norm, w_kv,
           b_norm, b_w_in, b_w_out, rel_bias, final_norm):
    B, S, D = x.shape
    M = B * S
    h0 = _rmsnorm_bf16(x.reshape(M, D), a_norm[0])
    u, sz0 = _inproj_a(h0, a_w_in[0])
    E = u.shape[-1]
    x1, hk, hb = _conv_out(u.reshape(B, S, E), sz0.reshape(B, S, E), x, a_conv_w[0], a_conv_b[0],
                           a_ln_g[0], a_ln_b[0], _cast_bf16(a_w_out[0]), kv_norm, b_norm[0])
    A = N_HEADS * HEAD_DIM
    qs, kvs = [], []
    for g, (_, dil) in enumerate(DILATED_GROUPS):
        qs.append(_proj(hb, b_w_in[0], g * A, A, dil, BF16, "scale", HEAD_DIM ** -0.5))
        kvs.append(_proj(hk, w_kv, 2 * g * A, 2 * A, dil, BF16))
    sz1 = _proj(hb, b_w_in[0], N_GROUPS * A, A, 1, F32, "silu")
    y = _attention(rel_bias, qs, kvs, sz1)
    out = _final(y.reshape(M, A), _cast_bf16(b_w_out[0]), x1.reshape(M, D), final_norm)
    return out.reshape(B, S, D)
```

```python
import functools
import math

import jax
import jax.numpy as jnp
from jax import lax
from jax.experimental import pallas as pl
from jax.experimental.pallas import tpu as pltpu

D_MODEL = 2048
CONV_WIDTH = 31
HEAD_DIM = 128
N_HEADS = D_MODEL // HEAD_DIM
DILATED_GROUPS = ((128, 1), (512, 4), (2048, 16))
N_GROUPS = len(DILATED_GROUPS)
BLOCK = 128
N_BUCKETS = 32
MAX_EXACT = N_BUCKETS // 2
MAX_DISTANCE = 2048
EPS = 1e-6

LANES = 128
SUBLANES = 8
HALO = 32

F32 = jnp.float32
BF16 = jnp.bfloat16
VMEM_LIMIT = 56 * 1024 * 1024
LOG2E = math.log2(math.e)


def _params(sem):
    return pltpu.CompilerParams(dimension_semantics=sem, vmem_limit_bytes=VMEM_LIMIT)


def _rmsnorm_kernel(x_ref, g_ref, o_ref):
    x = x_ref[...]
    ms = jnp.mean(x * x, axis=-1, keepdims=True)
    o_ref[...] = (x * lax.rsqrt(ms + EPS) * g_ref[...]).astype(o_ref.dtype)


def _rmsnorm_bf16(x2d, g, tm=512):
    M, D = x2d.shape
    return pl.pallas_call(
        _rmsnorm_kernel,
        grid=(M // tm,),
        in_specs=[pl.BlockSpec((tm, D), lambda i: (i, 0)),
                  pl.BlockSpec((1, D), lambda i: (0, 0))],
        out_specs=pl.BlockSpec((tm, D), lambda i: (i, 0)),
        out_shape=jax.ShapeDtypeStruct((M, D), BF16),
        compiler_params=_params(("arbitrary",)),
        name="prenorm",
    )(x2d, g.reshape(1, D))


def _cast_kernel(w_ref, o_ref):
    o_ref[...] = w_ref[...].astype(o_ref.dtype)


def _cast_bf16(w, tr=512):
    R, C = w.shape
    return pl.pallas_call(
        _cast_kernel,
        grid=(R // tr,),
        in_specs=[pl.BlockSpec((tr, C), lambda i: (i, 0))],
        out_specs=pl.BlockSpec((tr, C), lambda i: (i, 0)),
        out_shape=jax.ShapeDtypeStruct((R, C), BF16),
        compiler_params=_params(("arbitrary",)),
        name="cast_bf16",
    )(w)


def _inproj_a_kernel(h_ref, wa_ref, wb_ref, wz_ref, u_ref, sz_ref, wa_s, wb_s, wz_s):
    @pl.when(pl.program_id(1) == 0)
    def _():
        wa_s[...] = wa_ref[...].astype(BF16)
        wb_s[...] = wb_ref[...].astype(BF16)
        wz_s[...] = wz_ref[...].astype(BF16)

    h = h_ref[...]
    a = jnp.dot(h, wa_s[...], preferred_element_type=F32)
    b = jnp.dot(h, wb_s[...], preferred_element_type=F32)
    u_ref[...] = a * jax.nn.sigmoid(b)
    z = jnp.dot(h, wz_s[...], preferred_element_type=F32)
    sz_ref[...] = z * jax.nn.sigmoid(z)


def _inproj_a(h, w_in, tm=1024, tn=512):
    M, D = h.shape
    E = w_in.shape[1] // 3
    nb = E // tn
    return pl.pallas_call(
        _inproj_a_kernel,
        grid=(nb, M // tm),
        in_specs=[pl.BlockSpec((tm, D), lambda n, m: (m, 0)),
                  pl.BlockSpec((D, tn), lambda n, m: (0, n)),
                  pl.BlockSpec((D, tn), lambda n, m: (0, n + nb)),
                  pl.BlockSpec((D, tn), lambda n, m: (0, n + 2 * nb))],
        out_specs=[pl.BlockSpec((tm, tn), lambda n, m: (m, n)),
                   pl.BlockSpec((tm, tn), lambda n, m: (m, n))],
        out_shape=[jax.ShapeDtypeStruct((M, E), F32),
                   jax.ShapeDtypeStruct((M, E), F32)],
        scratch_shapes=[pltpu.VMEM((D, tn), BF16)] * 3,
        compiler_params=_params(("arbitrary", "arbitrary")),
        name="inproj_a",
    )(h, w_in, w_in, w_in)


def _conv_out_kernel(ucur_ref, uprev_ref, sz_ref, x_ref, cw_ref, cb_ref, lg_ref, lb_ref,
                     wo_ref, gkv_ref, gb_ref, x1_ref, hk_ref, hb_ref, seg_s, c_s, *, tm, pitch, oc):
    E = ucur_ref.shape[-1]
    A = tm // SUBLANES
    first = pl.program_id(1) == 0
    base = HALO - (CONV_WIDTH - 1)
    for ct in range(E // LANES):
        cs = slice(ct * LANES, (ct + 1) * LANES)
        seg_s[ct, 0:HALO, :] = jnp.where(first, 0.0, uprev_ref[0, :, cs])
        seg_s[ct, HALO:HALO + A, :] = ucur_ref[0, 0:A, cs]
        for i in range(1, SUBLANES):
            seg_s[ct, i * pitch:i * pitch + HALO + A, :] = ucur_ref[0, i * A - HALO:(i + 1) * A, cs]
        wv = [cw_ref[ct, pl.ds(j, SUBLANES, stride=0), :] for j in range(CONV_WIDTH)]
        bias = cb_ref[ct, pl.ds(0, SUBLANES, stride=0), :]
        for a0 in range(0, A, oc):
            accs = [None] * oc
            for k in range(oc + CONV_WIDTH - 1):
                w = seg_s[ct, pl.ds(a0 + base + k, SUBLANES, stride=pitch), :]
                for o in range(oc):
                    j = k - o
                    if 0 <= j < CONV_WIDTH:
                        t = w * wv[j]
                        accs[o] = t if accs[o] is None else accs[o] + t
            for o in range(oc):
                c_s[ct, pl.ds(a0 + o, SUBLANES, stride=A), :] = accs[o] + bias

    c = jnp.concatenate([c_s[ct] for ct in range(E // LANES)], axis=-1)
    mu = jnp.mean(c, axis=-1, keepdims=True)
    cc = c - mu
    var = jnp.mean(cc * cc, axis=-1, keepdims=True)
    cn = cc * lax.rsqrt(var + EPS) * lg_ref[...] + lb_ref[...]
    y = (cn * jax.nn.sigmoid(cn)) * sz_ref[0]
    o = jnp.dot(y.astype(BF16), wo_ref[...], preferred_element_type=F32)
    x1 = x_ref[0] + o
    x1_ref[0] = x1
    xn = x1 * lax.rsqrt(jnp.mean(x1 * x1, axis=-1, keepdims=True) + EPS)
    hk_ref[0] = (xn * gkv_ref[...]).astype(BF16)
    hb_ref[0] = (xn * gb_ref[...]).astype(BF16)


def _conv_out(u, sz, x, conv_w, conv_b, ln_g, ln_b, w_out_bf16, g_kv, g_b, tm=256, oc=8):
    B, S, E = u.shape
    D = x.shape[-1]
    nct = E // LANES
    pitch = HALO + tm // SUBLANES + SUBLANES
    row = lambda a: a.reshape(1, -1)
    full = lambda shape: pl.BlockSpec(shape, lambda b, i: (0,) * len(shape))
    tile = lambda w: pl.BlockSpec((1, tm, w), lambda b, i: (b, i, 0))
    hpb = tm // HALO
    return pl.pallas_call(
        functools.partial(_conv_out_kernel, tm=tm, pitch=pitch, oc=oc),
        grid=(B, S // tm),
        in_specs=[tile(E),
                  pl.BlockSpec((1, HALO, E), lambda b, i: (b, jnp.maximum(i * hpb - 1, 0), 0)),
                  tile(E), tile(D),
                  full((nct, CONV_WIDTH, LANES)), full((nct, 1, LANES)), full((1, E)), full((1, E)),
                  full((E, D)), full((1, D)), full((1, D))],
        out_specs=[tile(D), tile(D), tile(D)],
        out_shape=[jax.ShapeDtypeStruct((B, S, D), F32),
                   jax.ShapeDtypeStruct((B, S, D), BF16),
                   jax.ShapeDtypeStruct((B, S, D), BF16)],
        scratch_shapes=[pltpu.VMEM((nct, SUBLANES * pitch, LANES), F32),
                        pltpu.VMEM((nct, tm, LANES), F32)],
        compiler_params=_params(("arbitrary", "arbitrary")),
        name="conv_out",
    )(u, u, sz, x, conv_w.reshape(CONV_WIDTH, nct, LANES).transpose(1, 0, 2),
      conv_b.reshape(nct, 1, LANES), row(ln_g), row(ln_b), w_out_bf16, row(g_kv), row(g_b))


def _row_pitch(dil):
    return dil + SUBLANES if dil % (2 * SUBLANES) == 0 else dil


def _proj_kernel(x_ref, w_ref, o_ref, w_s, acc_s, *, dil, epilogue, scale):
    first = (pl.program_id(1) == 0) & (pl.program_id(2) == 0)

    @pl.when(first)
    def _():
        w_s[...] = w_ref[...].astype(BF16)

    acc = jnp.dot(x_ref[0], w_s[...], preferred_element_type=F32)
    if epilogue == "silu":
        acc = acc * jax.nn.sigmoid(acc)
    elif epilogue == "scale":
        acc = acc * scale
    nh = o_ref.shape[1]
    tl = o_ref.shape[3]
    if dil == 1:
        for hh in range(nh):
            o_ref[0, hh, 0] = acc[:, hh * HEAD_DIM:(hh + 1) * HEAD_DIM].astype(o_ref.dtype)
    else:
        pitch = _row_pitch(dil)
        for hh in range(nh):
            piece = acc[:, hh * HEAD_DIM:(hh + 1) * HEAD_DIM]
            if pitch == dil:
                acc_s[hh] = piece
            else:
                for l in range(tl):
                    acc_s[hh, l * pitch:l * pitch + dil, :] = piece[l * dil:(l + 1) * dil, :]
        for hh in range(nh):
            for r in range(dil):
                o_ref[0, hh, r] = acc_s[hh, pl.ds(r, tl, stride=pitch), :].astype(o_ref.dtype)


def _proj(act, w, col_off, n_cols, dil, out_dtype, epilogue="none", scale=1.0, tn=1024, tm=1024):
    B, S, D = act.shape
    L = S // dil
    tl = tm // dil
    nh = tn // HEAD_DIM
    cb0 = col_off // tn
    out = pl.pallas_call(
        functools.partial(_proj_kernel, dil=dil, epilogue=epilogue, scale=scale),
        grid=(n_cols // tn, B, S // tm),
        in_specs=[pl.BlockSpec((1, tm, D), lambda n, b, i: (b, i, 0)),
                  pl.BlockSpec((D, tn), lambda n, b, i: (0, cb0 + n))],
        out_specs=pl.BlockSpec((1, nh, dil, tl, HEAD_DIM), lambda n, b, i: (b, n, 0, i, 0)),
        out_shape=jax.ShapeDtypeStruct((B, n_cols // HEAD_DIM, dil, L, HEAD_DIM), out_dtype),
        scratch_shapes=[pltpu.VMEM((D, tn), BF16),
                        pltpu.VMEM((nh, tl * _row_pitch(dil), HEAD_DIM), F32)],
        compiler_params=_params(("arbitrary",) * 3),
        name="proj_d%d_%s" % (dil, epilogue),
    )(act, w)
    return out.reshape(B, n_cols // HEAD_DIM, S, HEAD_DIM)


def _bucket_tables():
    delta = (jnp.arange(BLOCK)[:, None] + BLOCK) - jnp.arange(2 * BLOCK)[None, :]
    tabs = []
    for window, dil in DILATED_GROUPS:
        w_sub = window // dil
        local = (delta >= 0) & (delta <= w_sub)
        dist = jnp.clip(delta, 0) * dil
        large = MAX_EXACT + (jnp.log(jnp.maximum(dist, 1).astype(F32) / MAX_EXACT)
                             / math.log(MAX_DISTANCE / MAX_EXACT)
                             * (N_BUCKETS - MAX_EXACT)).astype(jnp.int32)
        large = jnp.minimum(large, N_BUCKETS - 1)
        bucket = jnp.where(dist < MAX_EXACT, dist, large)
        tabs.append(jnp.where(local, bucket, -1).astype(jnp.int32))
    return jnp.stack(tabs)


def _attn_kernel(rb_ref, bkt_ref, q1, q2, q3, k1, k2, k3, v1, v2, v3, sz_ref, y_ref,
                 tbl_s, o_s, m_s, d_s, e_s, *, S, mc, unroll, tile):
    h = pl.program_id(0)

    @pl.when(pl.program_id(1) == 0)
    def _():
        for g in range(N_GROUPS):
            bk = bkt_ref[g]
            t = jnp.full(bk.shape, -jnp.inf, F32)
            for kk in range(N_BUCKETS):
                t = jnp.where(bk == kk, rb_ref[kk * N_HEADS + h] * LOG2E, t)
            tbl_s[g] = t

    groups = ((q1, k1, v1), (q2, k2, v2), (q3, k3, v3))
    bpt = tile // BLOCK

    blocks = [(g, j) for g in range(N_GROUPS) for j in range(bpt)]
    n_sets = len(blocks) // unroll

    def tile_body(T):
        def geometry(g, j):
            dil = DILATED_GROUPS[g][1]
            nb = S // BLOCK // dil
            bpr = bpt // dil
            r, jl = j // bpr, j % bpr
            lb = T * bpr + jl
            p = r * nb + lb
            nk = 1 if lb == 0 else 2
            row = p * BLOCK
            krow = (p + 1 - nk) * BLOCK
            t0 = jl * (BLOCK * dil) + r
            idx = pl.ds(t0, BLOCK) if dil == 1 else pl.ds(t0, BLOCK, stride=dil)
            return row, krow, nk, idx

        def scores(g, j, slot):
            row, krow, nk, idx = geometry(g, j)
            q = groups[g][0][0, 0, pl.ds(row, BLOCK), :]
            k = groups[g][1][0, 0, pl.ds(krow, nk * BLOCK), :]
            s = lax.dot_general(q, k, (((1,), (1,)), ((), ())), preferred_element_type=F32)
            s = s + tbl_s[g, :, (2 - nk) * BLOCK:]
            m = jnp.max(s, axis=-1, keepdims=True)
            e_s[slot, :, 0:nk * BLOCK] = jnp.exp2(s - m).astype(BF16)
            m_s[g, idx, :] = jnp.broadcast_to(m, (BLOCK, HEAD_DIM))

        def values(g, j, slot):
            row, krow, nk, idx = geometry(g, j)
            v = groups[g][2][0, 0, pl.ds(krow, nk * BLOCK), :]
            va = jnp.concatenate([v, jnp.ones_like(v)], axis=1)
            acc = jnp.dot(e_s[slot, :, 0:nk * BLOCK], va, preferred_element_type=F32)
            o_s[g, idx, :] = acc[:, :HEAD_DIM]
            d_s[g, idx, :] = acc[:, HEAD_DIM:]

        for k in range(n_sets + 1):
            for uu in range(unroll):
                if k < n_sets:
                    scores(*blocks[k * unroll + uu], (k % 2) * unroll + uu)
                if k > 0:
                    values(*blocks[(k - 1) * unroll + uu], ((k - 1) % 2) * unroll + uu)

        def merge(c, c2):
            rows = pl.ds(pl.multiple_of(c * mc, mc), mc)
            m1, m2, m3 = m_s[0, rows, :], m_s[1, rows, :], m_s[2, rows, :]
            mx = jnp.maximum(jnp.maximum(m1, m2), m3)
            w1, w2, w3 = jnp.exp2(m1 - mx), jnp.exp2(m2 - mx), jnp.exp2(m3 - mx)
            num = w1 * o_s[0, rows, :] + w2 * o_s[1, rows, :] + w3 * o_s[2, rows, :]
            den = w1 * d_s[0, rows, :] + w2 * d_s[1, rows, :] + w3 * d_s[2, rows, :]
            orow = pl.ds(pl.multiple_of(T * tile + c * mc, mc), mc)
            y_ref[0, orow, :] = ((num / den) * sz_ref[0, 0, orow, :]).astype(y_ref.dtype)
            return c2

        lax.fori_loop(0, tile // mc, merge, 0)

    for T in range(S // tile):
        tile_body(T)


def _attention(rel_bias, qs, kvs, sz, mc=128, unroll=4, tile=2048):
    B, H, S, Dh = qs[0].shape
    bkt = _bucket_tables()
    hm = lambda off: pl.BlockSpec((1, 1, S, Dh), lambda h, b: (b, h + off, 0, 0))
    in_specs = ([pl.BlockSpec(memory_space=pltpu.SMEM),
                 pl.BlockSpec(bkt.shape, lambda h, b: (0, 0, 0))]
                + [hm(0)] * 3 + [hm(0)] * 3 + [hm(H)] * 3 + [hm(0)])
    return pl.pallas_call(
        functools.partial(_attn_kernel, S=S, mc=mc, unroll=unroll, tile=tile),
        grid=(H, B),
        in_specs=in_specs,
        out_specs=pl.BlockSpec((1, S, Dh), lambda h, b: (b, 0, h)),
        out_shape=jax.ShapeDtypeStruct((B, S, H * Dh), BF16),
        scratch_shapes=[pltpu.VMEM((N_GROUPS, BLOCK, 2 * BLOCK), F32),
                        pltpu.VMEM((N_GROUPS, tile, Dh), F32),
                        pltpu.VMEM((N_GROUPS, tile, Dh), F32),
                        pltpu.VMEM((N_GROUPS, tile, Dh), F32),
                        pltpu.VMEM((2 * unroll, BLOCK, 2 * BLOCK), BF16)],
        compiler_params=_params(("arbitrary", "arbitrary")),
        name="attention",
    )(rel_bias.reshape(-1), bkt, *qs, *kvs, *kvs, sz)


def _final_kernel(y_ref, w_ref, x1_ref, g_ref, o_ref):
    x2 = x1_ref[...] + jnp.dot(y_ref[...], w_ref[...], preferred_element_type=F32)
    ms = jnp.mean(x2 * x2, axis=-1, keepdims=True)
    o_ref[...] = x2 * lax.rsqrt(ms + EPS) * g_ref[...]


def _final(y, w_bf16, x1, g, tm=512):
    M, E = y.shape
    D = w_bf16.shape[1]
    return pl.pallas_call(
        _final_kernel,
        grid=(M // tm,),
        in_specs=[pl.BlockSpec((tm, E), lambda i: (i, 0)),
                  pl.BlockSpec((E, D), lambda i: (0, 0)),
                  pl.BlockSpec((tm, D), lambda i: (i, 0)),
                  pl.BlockSpec((1, D), lambda i: (0, 0))],
        out_specs=pl.BlockSpec((tm, D), lambda i: (i, 0)),
        out_shape=jax.ShapeDtypeStruct((M, D), F32),
        compiler_params=_params(("arbitrary",)),
        name="outproj_final",
    )(y, w_bf16, x1, g.reshape(1, D))


def kernel(x, a_norm, a_w_in, a_conv_w, a_conv_b, a_ln_g, a_ln_b, a_w_out, kv_norm, w_kv,
           b_norm, b_w_in, b_w_out, rel_bias, final_norm):
    B, S, D = x.shape
    M = B * S
    h0 = _rmsnorm_bf16(x.reshape(M, D), a_norm[0])
    u, sz0 = _inproj_a(h0, a_w_in[0])
    E = u.shape[-1]
    x1, hk, hb = _conv_out(u.reshape(B, S, E), sz0.reshape(B, S, E), x, a_conv_w[0], a_conv_b[0],
                           a_ln_g[0], a_ln_b[0], _cast_bf16(a_w_out[0]), kv_norm, b_norm[0])
    A = N_HEADS * HEAD_DIM
    qs, kvs = [], []
    for g, (_, dil) in enumerate(DILATED_GROUPS):
        qs.append(_proj(hb, b_w_in[0], g * A, A, dil, BF16, "scale", HEAD_DIM ** -0.5 * LOG2E))
        kvs.append(_proj(hk, w_kv, 2 * g * A, 2 * A, dil, BF16))
    sz1 = _proj(hb, b_w_in[0], N_GROUPS * A, A, 1, F32, "silu")
    y = _attention(rel_bias, qs, kvs, sz1)
    out = _final(y.reshape(M, A), _cast_bf16(b_w_out[0]), x1.reshape(M, D), final_norm)
    return out.reshape(B, S, D)
```

```python
import functools
import math

import jax
import jax.numpy as jnp
from jax import lax
from jax.experimental import pallas as pl
from jax.experimental.pallas import tpu as pltpu

D_MODEL = 2048
CONV_WIDTH = 31
HEAD_DIM = 128
N_HEADS = D_MODEL // HEAD_DIM
DILATED_GROUPS = ((128, 1), (512, 4), (2048, 16))
N_GROUPS = len(DILATED_GROUPS)
BLOCK = 128
N_BUCKETS = 32
MAX_EXACT = N_BUCKETS // 2
MAX_DISTANCE = 2048
EPS = 1e-6

LANES = 128
SUBLANES = 8
HALO = 32

F32 = jnp.float32
BF16 = jnp.bfloat16
VMEM_LIMIT = 56 * 1024 * 1024
LOG2E = math.log2(math.e)


def _sigmoid(x):
    return 0.5 * jnp.tanh(0.5 * x) + 0.5


def _params(sem):
    return pltpu.CompilerParams(dimension_semantics=sem, vmem_limit_bytes=VMEM_LIMIT)


def _rmsnorm_kernel(x_ref, g_ref, o_ref):
    x = x_ref[...]
    ms = jnp.mean(x * x, axis=-1, keepdims=True)
    o_ref[...] = (x * lax.rsqrt(ms + EPS) * g_ref[...]).astype(o_ref.dtype)


def _rmsnorm_bf16(x2d, g, tm=512):
    M, D = x2d.shape
    return pl.pallas_call(
        _rmsnorm_kernel,
        grid=(M // tm,),
        in_specs=[pl.BlockSpec((tm, D), lambda i: (i, 0)),
                  pl.BlockSpec((1, D), lambda i: (0, 0))],
        out_specs=pl.BlockSpec((tm, D), lambda i: (i, 0)),
        out_shape=jax.ShapeDtypeStruct((M, D), BF16),
        compiler_params=_params(("arbitrary",)),
        name="prenorm",
    )(x2d, g.reshape(1, D))


def _cast_kernel(w_ref, o_ref):
    o_ref[...] = w_ref[...].astype(o_ref.dtype)


def _cast_bf16(w, tr=512):
    R, C = w.shape
    return pl.pallas_call(
        _cast_kernel,
        grid=(R // tr,),
        in_specs=[pl.BlockSpec((tr, C), lambda i: (i, 0))],
        out_specs=pl.BlockSpec((tr, C), lambda i: (i, 0)),
        out_shape=jax.ShapeDtypeStruct((R, C), BF16),
        compiler_params=_params(("arbitrary",)),
        name="cast_bf16",
    )(w)


def _inproj_a_kernel(h_ref, wa_ref, wb_ref, wz_ref, u_ref, sz_ref, wa_s, wb_s, wz_s):
    @pl.when(pl.program_id(1) == 0)
    def _():
        wa_s[...] = wa_ref[...].astype(BF16)
        wb_s[...] = wb_ref[...].astype(BF16)
        wz_s[...] = wz_ref[...].astype(BF16)

    h = h_ref[...]
    a = jnp.dot(h, wa_s[...], preferred_element_type=F32)
    b = jnp.dot(h, wb_s[...], preferred_element_type=F32)
    u_ref[...] = a * _sigmoid(b)
    z = jnp.dot(h, wz_s[...], preferred_element_type=F32)
    sz_ref[...] = z * _sigmoid(z)


def _inproj_a(h, w_in, tm=1024, tn=512):
    M, D = h.shape
    E = w_in.shape[1] // 3
    nb = E // tn
    return pl.pallas_call(
        _inproj_a_kernel,
        grid=(nb, M // tm),
        in_specs=[pl.BlockSpec((tm, D), lambda n, m: (m, 0)),
                  pl.BlockSpec((D, tn), lambda n, m: (0, n)),
                  pl.BlockSpec((D, tn), lambda n, m: (0, n + nb)),
                  pl.BlockSpec((D, tn), lambda n, m: (0, n + 2 * nb))],
        out_specs=[pl.BlockSpec((tm, tn), lambda n, m: (m, n)),
                   pl.BlockSpec((tm, tn), lambda n, m: (m, n))],
        out_shape=[jax.ShapeDtypeStruct((M, E), F32),
                   jax.ShapeDtypeStruct((M, E), F32)],
        scratch_shapes=[pltpu.VMEM((D, tn), BF16)] * 3,
        compiler_params=_params(("arbitrary", "arbitrary")),
        name="inproj_a",
    )(h, w_in, w_in, w_in)


def _conv_out_kernel(ucur_ref, uprev_ref, sz_ref, x_ref, cw_ref, cb_ref, lg_ref, lb_ref,
                     wo_ref, gkv_ref, gb_ref, x1_ref, hk_ref, hb_ref, seg_s, c_s, *, tm, pitch, cpitch, oc):
    E = ucur_ref.shape[-1]
    A = tm // SUBLANES
    first = pl.program_id(1) == 0
    base = HALO - (CONV_WIDTH - 1)
    for ct in range(E // LANES):
        cs = slice(ct * LANES, (ct + 1) * LANES)
        seg_s[ct, 0:HALO, :] = jnp.where(first, 0.0, uprev_ref[0, :, cs])
        seg_s[ct, HALO:HALO + A, :] = ucur_ref[0, 0:A, cs]
        for i in range(1, SUBLANES):
            seg_s[ct, i * pitch:i * pitch + HALO + A, :] = ucur_ref[0, i * A - HALO:(i + 1) * A, cs]
        wv = [cw_ref[ct, pl.ds(j, SUBLANES, stride=0), :] for j in range(CONV_WIDTH)]
        bias = cb_ref[ct, pl.ds(0, SUBLANES, stride=0), :]
        for a0 in range(0, A, oc):
            accs = [None] * oc
            for k in range(oc + CONV_WIDTH - 1):
                w = seg_s[ct, pl.ds(a0 + base + k, SUBLANES, stride=pitch), :]
                for o in range(oc):
                    j = k - o
                    if 0 <= j < CONV_WIDTH:
                        t = w * wv[j]
                        accs[o] = t if accs[o] is None else accs[o] + t
            for o in range(oc):
                c_s[ct, pl.ds(a0 + o, SUBLANES, stride=cpitch), :] = accs[o] + bias

    c = jnp.concatenate(
        [jnp.concatenate([c_s[ct, i * cpitch:i * cpitch + A, :] for i in range(SUBLANES)], axis=0)
         for ct in range(E // LANES)], axis=-1)
    mu = jnp.mean(c, axis=-1, keepdims=True)
    cc = c - mu
    var = jnp.mean(cc * cc, axis=-1, keepdims=True)
    cn = cc * lax.rsqrt(var + EPS) * lg_ref[...] + lb_ref[...]
    y = (cn * _sigmoid(cn)) * sz_ref[0]
    o = jnp.dot(y.astype(BF16), wo_ref[...], preferred_element_type=F32)
    x1 = x_ref[0] + o
    x1_ref[0] = x1
    xn = x1 * lax.rsqrt(jnp.mean(x1 * x1, axis=-1, keepdims=True) + EPS)
    hk_ref[0] = (xn * gkv_ref[...]).astype(BF16)
    hb_ref[0] = (xn * gb_ref[...]).astype(BF16)


def _conv_out(u, sz, x, conv_w, conv_b, ln_g, ln_b, w_out_bf16, g_kv, g_b, tm=256, oc=8):
    B, S, E = u.shape
    D = x.shape[-1]
    nct = E // LANES
    pitch = HALO + tm // SUBLANES + SUBLANES
    cpitch = tm // SUBLANES + SUBLANES
    row = lambda a: a.reshape(1, -1)
    full = lambda shape: pl.BlockSpec(shape, lambda b, i: (0,) * len(shape))
    tile = lambda w: pl.BlockSpec((1, tm, w), lambda b, i: (b, i, 0))
    hpb = tm // HALO
    return pl.pallas_call(
        functools.partial(_conv_out_kernel, tm=tm, pitch=pitch, cpitch=cpitch, oc=oc),
        grid=(B, S // tm),
        in_specs=[tile(E),
                  pl.BlockSpec((1, HALO, E), lambda b, i: (b, jnp.maximum(i * hpb - 1, 0), 0)),
                  tile(E), tile(D),
                  full((nct, CONV_WIDTH, LANES)), full((nct, 1, LANES)), full((1, E)), full((1, E)),
                  full((E, D)), full((1, D)), full((1, D))],
        out_specs=[tile(D), tile(D), tile(D)],
        out_shape=[jax.ShapeDtypeStruct((B, S, D), F32),
                   jax.ShapeDtypeStruct((B, S, D), BF16),
                   jax.ShapeDtypeStruct((B, S, D), BF16)],
        scratch_shapes=[pltpu.VMEM((nct, SUBLANES * pitch, LANES), F32),
                        pltpu.VMEM((nct, SUBLANES * cpitch, LANES), F32)],
        compiler_params=_params(("arbitrary", "arbitrary")),
        name="conv_out",
    )(u, u, sz, x, conv_w.reshape(CONV_WIDTH, nct, LANES).transpose(1, 0, 2),
      conv_b.reshape(nct, 1, LANES), row(ln_g), row(ln_b), w_out_bf16, row(g_kv), row(g_b))


def _row_pitch(dil):
    return dil + SUBLANES if dil % (2 * SUBLANES) == 0 else dil


def _proj_kernel(x_ref, w_ref, o_ref, w_s, acc_s, *, dil, epilogue, scale):
    first = (pl.program_id(1) == 0) & (pl.program_id(2) == 0)

    @pl.when(first)
    def _():
        w_s[...] = w_ref[...].astype(BF16)

    acc = jnp.dot(x_ref[0], w_s[...], preferred_element_type=F32)
    if epilogue == "silu":
        acc = acc * _sigmoid(acc)
    elif epilogue == "scale":
        acc = acc * scale
    nh = o_ref.shape[1]
    tl = o_ref.shape[3]
    if dil == 1:
        for hh in range(nh):
            o_ref[0, hh, 0] = acc[:, hh * HEAD_DIM:(hh + 1) * HEAD_DIM].astype(o_ref.dtype)
    else:
        pitch = _row_pitch(dil)
        for hh in range(nh):
            piece = acc[:, hh * HEAD_DIM:(hh + 1) * HEAD_DIM]
            if pitch == dil:
                acc_s[hh] = piece
            else:
                for l in range(tl):
                    acc_s[hh, l * pitch:l * pitch + dil, :] = piece[l * dil:(l + 1) * dil, :]
        for hh in range(nh):
            for r in range(dil):
                o_ref[0, hh, r] = acc_s[hh, pl.ds(r, tl, stride=pitch), :].astype(o_ref.dtype)


def _proj(act, w, col_off, n_cols, dil, out_dtype, epilogue="none", scale=1.0, tn=1024, tm=1024):
    B, S, D = act.shape
    L = S // dil
    tl = tm // dil
    nh = tn // HEAD_DIM
    cb0 = col_off // tn
    out = pl.pallas_call(
        functools.partial(_proj_kernel, dil=dil, epilogue=epilogue, scale=scale),
        grid=(n_cols // tn, B, S // tm),
        in_specs=[pl.BlockSpec((1, tm, D), lambda n, b, i: (b, i, 0)),
                  pl.BlockSpec((D, tn), lambda n, b, i: (0, cb0 + n))],
        out_specs=pl.BlockSpec((1, nh, dil, tl, HEAD_DIM), lambda n, b, i: (b, n, 0, i, 0)),
        out_shape=jax.ShapeDtypeStruct((B, n_cols // HEAD_DIM, dil, L, HEAD_DIM), out_dtype),
        scratch_shapes=[pltpu.VMEM((D, tn), BF16),
                        pltpu.VMEM((nh, tl * _row_pitch(dil), HEAD_DIM), F32)],
        compiler_params=_params(("arbitrary",) * 3),
        name="proj_d%d_%s" % (dil, epilogue),
    )(act, w)
    return out.reshape(B, n_cols // HEAD_DIM, S, HEAD_DIM)


def _bucket_tables():
    delta = (jnp.arange(BLOCK)[:, None] + BLOCK) - jnp.arange(2 * BLOCK)[None, :]
    tabs = []
    for window, dil in DILATED_GROUPS:
        w_sub = window // dil
        local = (delta >= 0) & (delta <= w_sub)
        dist = jnp.clip(delta, 0) * dil
        large = MAX_EXACT + (jnp.log(jnp.maximum(dist, 1).astype(F32) / MAX_EXACT)
                             / math.log(MAX_DISTANCE / MAX_EXACT)
                             * (N_BUCKETS - MAX_EXACT)).astype(jnp.int32)
        large = jnp.minimum(large, N_BUCKETS - 1)
        bucket = jnp.where(dist < MAX_EXACT, dist, large)
        tabs.append(jnp.where(local, bucket, -1).astype(jnp.int32))
    return jnp.stack(tabs)


def _attn_kernel(rb_ref, bkt_ref, q1, q2, q3, k1, k2, k3, v1, v2, v3, sz_ref, y_ref,
                 tbl_s, o_s, m_s, d_s, e_s, *, S, mc, unroll, tile):
    h = pl.program_id(0)

    @pl.when(pl.program_id(1) == 0)
    def _():
        for g in range(N_GROUPS):
            bk = bkt_ref[g]
            t = jnp.full(bk.shape, -jnp.inf, F32)
            for kk in range(N_BUCKETS):
                t = jnp.where(bk == kk, rb_ref[kk * N_HEADS + h] * LOG2E, t)
            tbl_s[g] = t

    groups = ((q1, k1, v1), (q2, k2, v2), (q3, k3, v3))
    bpt = tile // BLOCK

    blocks = [(g, j) for g in range(N_GROUPS) for j in range(bpt)]
    n_sets = len(blocks) // unroll

    def tile_body(T):
        def geometry(g, j):
            dil = DILATED_GROUPS[g][1]
            nb = S // BLOCK // dil
            bpr = bpt // dil
            r, jl = j // bpr, j % bpr
            lb = T * bpr + jl
            p = r * nb + lb
            nk = 1 if lb == 0 else 2
            row = p * BLOCK
            krow = (p + 1 - nk) * BLOCK
            t0 = jl * (BLOCK * dil) + r
            idx = pl.ds(t0, BLOCK) if dil == 1 else pl.ds(t0, BLOCK, stride=dil)
            return row, krow, nk, idx

        def scores(g, j, slot):
            row, krow, nk, idx = geometry(g, j)
            q = groups[g][0][0, 0, pl.ds(row, BLOCK), :]
            k = groups[g][1][0, 0, pl.ds(krow, nk * BLOCK), :]
            s = lax.dot_general(q, k, (((1,), (1,)), ((), ())), preferred_element_type=F32)
            s = s + tbl_s[g, :, (2 - nk) * BLOCK:]
            m = jnp.max(s, axis=-1, keepdims=True)
            e_s[slot, :, 0:nk * BLOCK] = jnp.exp2(s - m).astype(BF16)
            m_s[g, idx, :] = jnp.broadcast_to(m, (BLOCK, HEAD_DIM))

        def values(g, j, slot):
            row, krow, nk, idx = geometry(g, j)
            v = groups[g][2][0, 0, pl.ds(krow, nk * BLOCK), :]
            va = jnp.concatenate([v, jnp.ones_like(v)], axis=1)
            acc = jnp.dot(e_s[slot, :, 0:nk * BLOCK], va, preferred_element_type=F32)
            o_s[g, idx, :] = acc[:, :HEAD_DIM]
            d_s[g, idx, :] = acc[:, HEAD_DIM:]

        for k in range(n_sets + 1):
            for uu in range(unroll):
                if k < n_sets:
                    scores(*blocks[k * unroll + uu], (k % 2) * unroll + uu)
                if k > 0:
                    values(*blocks[(k - 1) * unroll + uu], ((k - 1) % 2) * unroll + uu)

        def merge(c, c2):
            rows = pl.ds(pl.multiple_of(c * mc, mc), mc)
            m1, m2, m3 = m_s[0, rows, :], m_s[1, rows, :], m_s[2, rows, :]
            mx = jnp.maximum(jnp.maximum(m1, m2), m3)
            w1, w2, w3 = jnp.exp2(m1 - mx), jnp.exp2(m2 - mx), jnp.exp2(m3 - mx)
            num = w1 * o_s[0, rows, :] + w2 * o_s[1, rows, :] + w3 * o_s[2, rows, :]
            den = w1 * d_s[0, rows, :] + w2 * d_s[1, rows, :] + w3 * d_s[2, rows, :]
            orow = pl.ds(pl.multiple_of(T * tile + c * mc, mc), mc)
            y_ref[0, orow, :] = ((num / den) * sz_ref[0, 0, orow, :]).astype(y_ref.dtype)
            return c2

        lax.fori_loop(0, tile // mc, merge, 0)

    for T in range(S // tile):
        tile_body(T)


def _attention(rel_bias, qs, kvs, sz, mc=128, unroll=4, tile=2048):
    B, H, S, Dh = qs[0].shape
    bkt = _bucket_tables()
    hm = lambda off: pl.BlockSpec((1, 1, S, Dh), lambda h, b: (b, h + off, 0, 0))
    in_specs = ([pl.BlockSpec(memory_space=pltpu.SMEM),
                 pl.BlockSpec(bkt.shape, lambda h, b: (0, 0, 0))]
                + [hm(0)] * 3 + [hm(0)] * 3 + [hm(H)] * 3 + [hm(0)])
    return pl.pallas_call(
        functools.partial(_attn_kernel, S=S, mc=mc, unroll=unroll, tile=tile),
        grid=(H, B),
        in_specs=in_specs,
        out_specs=pl.BlockSpec((1, S, Dh), lambda h, b: (b, 0, h)),
        out_shape=jax.ShapeDtypeStruct((B, S, H * Dh), BF16),
        scratch_shapes=[pltpu.VMEM((N_GROUPS, BLOCK, 2 * BLOCK), F32),
                        pltpu.VMEM((N_GROUPS, tile, Dh), F32),
                        pltpu.VMEM((N_GROUPS, tile, Dh), F32),
                        pltpu.VMEM((N_GROUPS, tile, Dh), F32),
                        pltpu.VMEM((2 * unroll, BLOCK, 2 * BLOCK), BF16)],
        compiler_params=_params(("arbitrary", "arbitrary")),
        name="attention",
    )(rel_bias.reshape(-1), bkt, *qs, *kvs, *kvs, sz)


def _final_kernel(y_ref, w_ref, x1_ref, g_ref, o_ref):
    x2 = x1_ref[...] + jnp.dot(y_ref[...], w_ref[...], preferred_element_type=F32)
    ms = jnp.mean(x2 * x2, axis=-1, keepdims=True)
    o_ref[...] = x2 * lax.rsqrt(ms + EPS) * g_ref[...]


def _final(y, w_bf16, x1, g, tm=512):
    M, E = y.shape
    D = w_bf16.shape[1]
    return pl.pallas_call(
        _final_kernel,
        grid=(M // tm,),
        in_specs=[pl.BlockSpec((tm, E), lambda i: (i, 0)),
                  pl.BlockSpec((E, D), lambda i: (0, 0)),
                  pl.BlockSpec((tm, D), lambda i: (i, 0)),
                  pl.BlockSpec((1, D), lambda i: (0, 0))],
        out_specs=pl.BlockSpec((tm, D), lambda i: (i, 0)),
        out_shape=jax.ShapeDtypeStruct((M, D), F32),
        compiler_params=_params(("arbitrary",)),
        name="outproj_final",
    )(y, w_bf16, x1, g.reshape(1, D))


def kernel(x, a_norm, a_w_in, a_conv_w, a_conv_b, a_ln_g, a_ln_b, a_w_out, kv_norm, w_kv,
           b_norm, b_w_in, b_w_out, rel_bias, final_norm):
    B, S, D = x.shape
    M = B * S
    h0 = _rmsnorm_bf16(x.reshape(M, D), a_norm[0])
    u, sz0 = _inproj_a(h0, a_w_in[0])
    E = u.shape[-1]
    x1, hk, hb = _conv_out(u.reshape(B, S, E), sz0.reshape(B, S, E), x, a_conv_w[0], a_conv_b[0],
                           a_ln_g[0], a_ln_b[0], _cast_bf16(a_w_out[0]), kv_norm, b_norm[0])
    A = N_HEADS * HEAD_DIM
    qs, kvs = [], []
    for g, (_, dil) in enumerate(DILATED_GROUPS):
        qs.append(_proj(hb, b_w_in[0], g * A, A, dil, BF16, "scale", HEAD_DIM ** -0.5 * LOG2E))
        kvs.append(_proj(hk, w_kv, 2 * g * A, 2 * A, dil, BF16))
    sz1 = _proj(hb, b_w_in[0], N_GROUPS * A, A, 1, F32, "silu")
    y = _attention(rel_bias, qs, kvs, sz1)
    out = _final(y.reshape(M, A), _cast_bf16(b_w_out[0]), x1.reshape(M, D), final_norm)
    return out.reshape(B, S, D)
```

```python
import functools
import math

import jax
import jax.numpy as jnp
from jax import lax
from jax.experimental import pallas as pl
from jax.experimental.pallas import tpu as pltpu

D_MODEL = 2048
CONV_WIDTH = 31
HEAD_DIM = 128
N_HEADS = D_MODEL // HEAD_DIM
DILATED_GROUPS = ((128, 1), (512, 4), (2048, 16))
N_GROUPS = len(DILATED_GROUPS)
BLOCK = 128
N_BUCKETS = 32
MAX_EXACT = N_BUCKETS // 2
MAX_DISTANCE = 2048
EPS = 1e-6

LANES = 128
SUBLANES = 8
HALO = 32

F32 = jnp.float32
BF16 = jnp.bfloat16
VMEM_LIMIT = 56 * 1024 * 1024
LOG2E = math.log2(math.e)


def _sigmoid(x):
    return 0.5 * jnp.tanh(0.5 * x) + 0.5


def _params(sem):
    return pltpu.CompilerParams(dimension_semantics=sem, vmem_limit_bytes=VMEM_LIMIT)


def _rmsnorm_kernel(x_ref, g_ref, o_ref):
    x = x_ref[...]
    ms = jnp.mean(x * x, axis=-1, keepdims=True)
    o_ref[...] = (x * lax.rsqrt(ms + EPS) * g_ref[...]).astype(o_ref.dtype)


def _rmsnorm_bf16(x2d, g, tm=1024):
    M, D = x2d.shape
    return pl.pallas_call(
        _rmsnorm_kernel,
        grid=(M // tm,),
        in_specs=[pl.BlockSpec((tm, D), lambda i: (i, 0)),
                  pl.BlockSpec((1, D), lambda i: (0, 0))],
        out_specs=pl.BlockSpec((tm, D), lambda i: (i, 0)),
        out_shape=jax.ShapeDtypeStruct((M, D), BF16),
        compiler_params=_params(("arbitrary",)),
        name="prenorm",
    )(x2d, g.reshape(1, D))


def _cast_kernel(w_ref, o_ref):
    o_ref[...] = w_ref[...].astype(o_ref.dtype)


def _cast_bf16(w, tr=512):
    R, C = w.shape
    return pl.pallas_call(
        _cast_kernel,
        grid=(R // tr,),
        in_specs=[pl.BlockSpec((tr, C), lambda i: (i, 0))],
        out_specs=pl.BlockSpec((tr, C), lambda i: (i, 0)),
        out_shape=jax.ShapeDtypeStruct((R, C), BF16),
        compiler_params=_params(("arbitrary",)),
        name="cast_bf16",
    )(w)


def _inproj_a_kernel(h_ref, wa_ref, wb_ref, wz_ref, u_ref, sz_ref, wa_s, wb_s, wz_s):
    @pl.when(pl.program_id(1) == 0)
    def _():
        wa_s[...] = wa_ref[...].astype(BF16)
        wb_s[...] = wb_ref[...].astype(BF16)
        wz_s[...] = wz_ref[...].astype(BF16)

    h = h_ref[...]
    a = jnp.dot(h, wa_s[...], preferred_element_type=F32)
    b = jnp.dot(h, wb_s[...], preferred_element_type=F32)
    u_ref[...] = a * _sigmoid(b)
    z = jnp.dot(h, wz_s[...], preferred_element_type=F32)
    sz_ref[...] = z * _sigmoid(z)


def _inproj_a(h, w_in, tm=1024, tn=512):
    M, D = h.shape
    E = w_in.shape[1] // 3
    nb = E // tn
    return pl.pallas_call(
        _inproj_a_kernel,
        grid=(nb, M // tm),
        in_specs=[pl.BlockSpec((tm, D), lambda n, m: (m, 0)),
                  pl.BlockSpec((D, tn), lambda n, m: (0, n)),
                  pl.BlockSpec((D, tn), lambda n, m: (0, n + nb)),
                  pl.BlockSpec((D, tn), lambda n, m: (0, n + 2 * nb))],
        out_specs=[pl.BlockSpec((tm, tn), lambda n, m: (m, n)),
                   pl.BlockSpec((tm, tn), lambda n, m: (m, n))],
        out_shape=[jax.ShapeDtypeStruct((M, E), F32),
                   jax.ShapeDtypeStruct((M, E), F32)],
        scratch_shapes=[pltpu.VMEM((D, tn), BF16)] * 3,
        compiler_params=_params(("arbitrary", "arbitrary")),
        name="inproj_a",
    )(h, w_in, w_in, w_in)


def _conv_out_kernel(ucur_ref, uprev_ref, sz_ref, x_ref, cw_ref, cb_ref, lg_ref, lb_ref,
                     wo_ref, gkv_ref, gb_ref, x1_ref, hk_ref, hb_ref, seg_s, c_s, *, tm, pitch, cpitch, oc):
    E = ucur_ref.shape[-1]
    A = tm // SUBLANES
    first = pl.program_id(1) == 0
    base = HALO - (CONV_WIDTH - 1)
    for ct in range(E // LANES):
        cs = slice(ct * LANES, (ct + 1) * LANES)
        seg_s[ct, 0:HALO, :] = jnp.where(first, 0.0, uprev_ref[0, :, cs])
        seg_s[ct, HALO:HALO + A, :] = ucur_ref[0, 0:A, cs]
        for i in range(1, SUBLANES):
            seg_s[ct, i * pitch:i * pitch + HALO + A, :] = ucur_ref[0, i * A - HALO:(i + 1) * A, cs]
        wv = [cw_ref[ct, pl.ds(j, SUBLANES, stride=0), :] for j in range(CONV_WIDTH)]
        bias = cb_ref[ct, pl.ds(0, SUBLANES, stride=0), :]
        for a0 in range(0, A, oc):
            accs = [None] * oc
            for k in range(oc + CONV_WIDTH - 1):
                w = seg_s[ct, pl.ds(a0 + base + k, SUBLANES, stride=pitch), :]
                for o in range(oc):
                    j = k - o
                    if 0 <= j < CONV_WIDTH:
                        t = w * wv[j]
                        accs[o] = t if accs[o] is None else accs[o] + t
            for o in range(oc):
                c_s[ct, pl.ds(a0 + o, SUBLANES, stride=cpitch), :] = accs[o] + bias

    c = jnp.concatenate(
        [jnp.concatenate([c_s[ct, i * cpitch:i * cpitch + A, :] for i in range(SUBLANES)], axis=0)
         for ct in range(E // LANES)], axis=-1)
    mu = jnp.mean(c, axis=-1, keepdims=True)
    cc = c - mu
    var = jnp.mean(cc * cc, axis=-1, keepdims=True)
    cn = cc * lax.rsqrt(var + EPS) * lg_ref[...] + lb_ref[...]
    y = (cn * _sigmoid(cn)) * sz_ref[0]
    o = jnp.dot(y.astype(BF16), wo_ref[...], preferred_element_type=F32)
    x1 = x_ref[0] + o
    x1_ref[0] = x1
    xn = x1 * lax.rsqrt(jnp.mean(x1 * x1, axis=-1, keepdims=True) + EPS)
    hk_ref[0] = (xn * gkv_ref[...]).astype(BF16)
    hb_ref[0] = (xn * gb_ref[...]).astype(BF16)


def _conv_out(u, sz, x, conv_w, conv_b, ln_g, ln_b, w_out_bf16, g_kv, g_b, tm=256, oc=8):
    B, S, E = u.shape
    D = x.shape[-1]
    nct = E // LANES
    pitch = HALO + tm // SUBLANES + SUBLANES
    cpitch = tm // SUBLANES + SUBLANES
    row = lambda a: a.reshape(1, -1)
    full = lambda shape: pl.BlockSpec(shape, lambda b, i: (0,) * len(shape))
    tile = lambda w: pl.BlockSpec((1, tm, w), lambda b, i: (b, i, 0))
    hpb = tm // HALO
    return pl.pallas_call(
        functools.partial(_conv_out_kernel, tm=tm, pitch=pitch, cpitch=cpitch, oc=oc),
        grid=(B, S // tm),
        in_specs=[tile(E),
                  pl.BlockSpec((1, HALO, E), lambda b, i: (b, jnp.maximum(i * hpb - 1, 0), 0)),
                  tile(E), tile(D),
                  full((nct, CONV_WIDTH, LANES)), full((nct, 1, LANES)), full((1, E)), full((1, E)),
                  full((E, D)), full((1, D)), full((1, D))],
        out_specs=[tile(D), tile(D), tile(D)],
        out_shape=[jax.ShapeDtypeStruct((B, S, D), F32),
                   jax.ShapeDtypeStruct((B, S, D), BF16),
                   jax.ShapeDtypeStruct((B, S, D), BF16)],
        scratch_shapes=[pltpu.VMEM((nct, SUBLANES * pitch, LANES), F32),
                        pltpu.VMEM((nct, SUBLANES * cpitch, LANES), F32)],
        compiler_params=_params(("arbitrary", "arbitrary")),
        name="conv_out",
    )(u, u, sz, x, conv_w.reshape(CONV_WIDTH, nct, LANES).transpose(1, 0, 2),
      conv_b.reshape(nct, 1, LANES), row(ln_g), row(ln_b), w_out_bf16, row(g_kv), row(g_b))


def _row_pitch(dil):
    return dil + SUBLANES if dil % (2 * SUBLANES) == 0 else dil


def _proj_kernel(x_ref, w_ref, o_ref, w_s, acc_s, *, dil, epilogue, scale):
    first = (pl.program_id(1) == 0) & (pl.program_id(2) == 0)

    @pl.when(first)
    def _():
        w_s[...] = w_ref[...].astype(BF16)

    acc = jnp.dot(x_ref[0], w_s[...], preferred_element_type=F32)
    if epilogue == "silu":
        acc = acc * _sigmoid(acc)
    elif epilogue == "scale":
        acc = acc * scale
    nh = o_ref.shape[1]
    tl = o_ref.shape[3]
    if dil == 1:
        for hh in range(nh):
            o_ref[0, hh, 0] = acc[:, hh * HEAD_DIM:(hh + 1) * HEAD_DIM].astype(o_ref.dtype)
    else:
        pitch = _row_pitch(dil)
        for hh in range(nh):
            piece = acc[:, hh * HEAD_DIM:(hh + 1) * HEAD_DIM]
            if pitch == dil:
                acc_s[hh] = piece
            else:
                for l in range(tl):
                    acc_s[hh, l * pitch:l * pitch + dil, :] = piece[l * dil:(l + 1) * dil, :]
        for hh in range(nh):
            for r in range(dil):
                o_ref[0, hh, r] = acc_s[hh, pl.ds(r, tl, stride=pitch), :].astype(o_ref.dtype)


def _proj(act, w, col_off, n_cols, dil, out_dtype, epilogue="none", scale=1.0, tn=1024, tm=1024):
    B, S, D = act.shape
    L = S // dil
    tl = tm // dil
    nh = tn // HEAD_DIM
    cb0 = col_off // tn
    out = pl.pallas_call(
        functools.partial(_proj_kernel, dil=dil, epilogue=epilogue, scale=scale),
        grid=(n_cols // tn, B, S // tm),
        in_specs=[pl.BlockSpec((1, tm, D), lambda n, b, i: (b, i, 0)),
                  pl.BlockSpec((D, tn), lambda n, b, i: (0, cb0 + n))],
        out_specs=pl.BlockSpec((1, nh, dil, tl, HEAD_DIM), lambda n, b, i: (b, n, 0, i, 0)),
        out_shape=jax.ShapeDtypeStruct((B, n_cols // HEAD_DIM, dil, L, HEAD_DIM), out_dtype),
        scratch_shapes=[pltpu.VMEM((D, tn), BF16),
                        pltpu.VMEM((nh, tl * _row_pitch(dil), HEAD_DIM), F32)],
        compiler_params=_params(("arbitrary",) * 3),
        name="proj_d%d_%s" % (dil, epilogue),
    )(act, w)
    return out.reshape(B, n_cols // HEAD_DIM, S, HEAD_DIM)


def _bucket_tables():
    delta = (jnp.arange(BLOCK)[:, None] + BLOCK) - jnp.arange(2 * BLOCK)[None, :]
    tabs = []
    for window, dil in DILATED_GROUPS:
        w_sub = window // dil
        local = (delta >= 0) & (delta <= w_sub)
        dist = jnp.clip(delta, 0) * dil
        large = MAX_EXACT + (jnp.log(jnp.maximum(dist, 1).astype(F32) / MAX_EXACT)
                             / math.log(MAX_DISTANCE / MAX_EXACT)
                             * (N_BUCKETS - MAX_EXACT)).astype(jnp.int32)
        large = jnp.minimum(large, N_BUCKETS - 1)
        bucket = jnp.where(dist < MAX_EXACT, dist, large)
        tabs.append(jnp.where(local, bucket, -1).astype(jnp.int32))
    return jnp.stack(tabs)


def _attn_kernel(rb_ref, bkt_ref, q1, q2, q3, k1, k2, k3, v1, v2, v3, sz_ref, y_ref,
                 tbl_s, o_s, m_s, d_s, e_s, *, S, mc, unroll, tile):
    h = pl.program_id(0)

    @pl.when(pl.program_id(1) == 0)
    def _():
        for g in range(N_GROUPS):
            bk = bkt_ref[g]
            t = jnp.full(bk.shape, -jnp.inf, F32)
            for kk in range(N_BUCKETS):
                t = jnp.where(bk == kk, rb_ref[kk * N_HEADS + h] * LOG2E, t)
            tbl_s[g] = t

    groups = ((q1, k1, v1), (q2, k2, v2), (q3, k3, v3))
    bpt = tile // BLOCK

    blocks = [(g, j) for g in range(N_GROUPS) for j in range(bpt)]
    n_sets = len(blocks) // unroll

    def tile_body(T):
        def geometry(g, j):
            dil = DILATED_GROUPS[g][1]
            nb = S // BLOCK // dil
            bpr = bpt // dil
            r, jl = j // bpr, j % bpr
            lb = T * bpr + jl
            p = r * nb + lb
            nk = 1 if lb == 0 else 2
            row = p * BLOCK
            krow = (p + 1 - nk) * BLOCK
            pitch = _row_pitch(dil)
            t0 = jl * (BLOCK * pitch) + r
            idx = pl.ds(t0, BLOCK) if dil == 1 else pl.ds(t0, BLOCK, stride=pitch)
            return row, krow, nk, idx

        def scores(g, j, slot):
            row, krow, nk, idx = geometry(g, j)
            q = groups[g][0][0, 0, pl.ds(row, BLOCK), :]
            k = groups[g][1][0, 0, pl.ds(krow, nk * BLOCK), :]
            s = lax.dot_general(q, k, (((1,), (1,)), ((), ())), preferred_element_type=F32)
            s = s + tbl_s[g, :, (2 - nk) * BLOCK:]
            m = jnp.max(s, axis=-1, keepdims=True)
            e_s[slot, :, 0:nk * BLOCK] = jnp.exp2(s - m).astype(BF16)
            m_s[g, idx, :] = jnp.broadcast_to(m, (BLOCK, HEAD_DIM))

        def values(g, j, slot):
            row, krow, nk, idx = geometry(g, j)
            v = groups[g][2][0, 0, pl.ds(krow, nk * BLOCK), :]
            va = jnp.concatenate([v, jnp.ones_like(v)], axis=1)
            acc = jnp.dot(e_s[slot, :, 0:nk * BLOCK], va, preferred_element_type=F32)
            o_s[g, idx, :] = acc[:, :HEAD_DIM]
            d_s[g, idx, :] = acc[:, HEAD_DIM:]

        for k in range(n_sets + 1):
            for uu in range(unroll):
                if k < n_sets:
                    scores(*blocks[k * unroll + uu], (k % 2) * unroll + uu)
                if k > 0:
                    values(*blocks[(k - 1) * unroll + uu], ((k - 1) % 2) * unroll + uu)

        def merge(c, c2):
            rows = pl.ds(pl.multiple_of(c * mc, mc), mc)
            def staged(ref, g):
                dil = DILATED_GROUPS[g][1]
                pitch = _row_pitch(dil)
                if pitch == dil:
                    return ref[g, rows, :]
                return jnp.concatenate(
                    [ref[g, pl.ds(pl.multiple_of((c * (mc // dil) + i) * pitch, SUBLANES), dil), :]
                     for i in range(mc // dil)], axis=0)

            m1, m2, m3 = staged(m_s, 0), staged(m_s, 1), staged(m_s, 2)
            mx = jnp.maximum(jnp.maximum(m1, m2), m3)
            w1, w2, w3 = jnp.exp2(m1 - mx), jnp.exp2(m2 - mx), jnp.exp2(m3 - mx)
            num = w1 * staged(o_s, 0) + w2 * staged(o_s, 1) + w3 * staged(o_s, 2)
            den = w1 * staged(d_s, 0) + w2 * staged(d_s, 1) + w3 * staged(d_s, 2)
            orow = pl.ds(pl.multiple_of(T * tile + c * mc, mc), mc)
            y_ref[0, 0, orow, :] = ((num / den) * sz_ref[0, 0, orow, :]).astype(y_ref.dtype)
            return c2

        lax.fori_loop(0, tile // mc, merge, 0)

    for T in range(S // tile):
        tile_body(T)


def _attention(rel_bias, qs, kvs, sz, mc=128, unroll=4, tile=2048):
    B, H, S, Dh = qs[0].shape
    bkt = _bucket_tables()
    rows = max(tile // dil * _row_pitch(dil) for _, dil in DILATED_GROUPS)
    hm =lambda off: pl.BlockSpec((1, 1, S, Dh), lambda h, b: (b, h + off, 0, 0))
    in_specs = ([pl.BlockSpec(memory_space=pltpu.SMEM),
                 pl.BlockSpec(bkt.shape, lambda h, b: (0, 0, 0))]
                + [hm(0)] * 3 + [hm(0)] * 3 + [hm(H)] * 3 + [hm(0)])
    return pl.pallas_call(
        functools.partial(_attn_kernel, S=S, mc=mc, unroll=unroll, tile=tile),
        grid=(H, B),
        in_specs=in_specs,
        out_specs=hm(0),
        out_shape=jax.ShapeDtypeStruct((B, H, S, Dh), BF16),
        scratch_shapes=[pltpu.VMEM((N_GROUPS, BLOCK, 2 * BLOCK), F32),
                        pltpu.VMEM((N_GROUPS, rows, Dh), F32),
                        pltpu.VMEM((N_GROUPS, rows, Dh), F32),
                        pltpu.VMEM((N_GROUPS, rows, Dh), F32),
                        pltpu.VMEM((2 * unroll, BLOCK, 2 * BLOCK), BF16)],
        compiler_params=_params(("arbitrary", "arbitrary")),
        name="attention",
    )(rel_bias.reshape(-1), bkt, *qs, *kvs, *kvs, sz)


def _final_kernel(y_ref, w_ref, x1_ref, g_ref, o_ref):
    y = jnp.concatenate([y_ref[0, h] for h in range(y_ref.shape[1])], axis=-1)
    x2 = x1_ref[0] + jnp.dot(y, w_ref[...], preferred_element_type=F32)
    ms = jnp.mean(x2 * x2, axis=-1, keepdims=True)
    o_ref[0] = x2 * lax.rsqrt(ms + EPS) * g_ref[...]


def _final(y, w_bf16, x1, g, tm=512):
    B, H, S, Dh = y.shape
    D = w_bf16.shape[1]
    return pl.pallas_call(
        _final_kernel,
        grid=(B, S // tm),
        in_specs=[pl.BlockSpec((1, H, tm, Dh), lambda b, i: (b, 0, i, 0)),
                  pl.BlockSpec((H * Dh, D), lambda b, i: (0, 0)),
                  pl.BlockSpec((1, tm, D), lambda b, i: (b, i, 0)),
                  pl.BlockSpec((1, D), lambda b, i: (0, 0))],
        out_specs=pl.BlockSpec((1, tm, D), lambda b, i: (b, i, 0)),
        out_shape=jax.ShapeDtypeStruct((B, S, D), F32),
        compiler_params=_params(("arbitrary", "arbitrary")),
        name="outproj_final",
    )(y, w_bf16, x1, g.reshape(1, D))


def kernel(x, a_norm, a_w_in, a_conv_w, a_conv_b, a_ln_g, a_ln_b, a_w_out, kv_norm, w_kv,
           b_norm, b_w_in, b_w_out, rel_bias, final_norm):
    B, S, D = x.shape
    M = B * S
    h0 = _rmsnorm_bf16(x.reshape(M, D), a_norm[0])
    u, sz0 = _inproj_a(h0, a_w_in[0])
    E = u.shape[-1]
    x1, hk, hb = _conv_out(u.reshape(B, S, E), sz0.reshape(B, S, E), x, a_conv_w[0], a_conv_b[0],
                           a_ln_g[0], a_ln_b[0], _cast_bf16(a_w_out[0]), kv_norm, b_norm[0])
    A = N_HEADS * HEAD_DIM
    qs, kvs = [], []
    for g, (_, dil) in enumerate(DILATED_GROUPS):
        qs.append(_proj(hb, b_w_in[0], g * A, A, dil, BF16, "scale", HEAD_DIM ** -0.5 * LOG2E))
        kvs.append(_proj(hk, w_kv, 2 * g * A, 2 * A, dil, BF16))
    sz1 = _proj(hb, b_w_in[0], N_GROUPS * A, A, 1, F32, "silu")
    y = _attention(rel_bias, qs, kvs, sz1)
    return _final(y, _cast_bf16(b_w_out[0]), x1, final_norm)
```

```python
import functools
import math

import jax
import jax.numpy as jnp
from jax import lax
from jax.experimental import pallas as pl
from jax.experimental.pallas import tpu as pltpu

D_MODEL = 2048
CONV_WIDTH = 31
HEAD_DIM = 128
N_HEADS = D_MODEL // HEAD_DIM
DILATED_GROUPS = ((128, 1), (512, 4), (2048, 16))
N_GROUPS = len(DILATED_GROUPS)
BLOCK = 128
N_BUCKETS = 32
MAX_EXACT = N_BUCKETS // 2
MAX_DISTANCE = 2048
EPS = 1e-6

LANES = 128
SUBLANES = 8
HALO = 32

F32 = jnp.float32
BF16 = jnp.bfloat16
VMEM_LIMIT = 56 * 1024 * 1024
LOG2E = math.log2(math.e)


def _sigmoid(x):
    return 0.5 * jnp.tanh(0.5 * x) + 0.5


def _params(sem):
    return pltpu.CompilerParams(dimension_semantics=sem, vmem_limit_bytes=VMEM_LIMIT)


def _rmsnorm_kernel(x_ref, g_ref, o_ref):
    x = x_ref[...]
    ms = jnp.mean(x * x, axis=-1, keepdims=True)
    o_ref[...] = (x * lax.rsqrt(ms + EPS) * g_ref[...]).astype(o_ref.dtype)


def _rmsnorm_bf16(x2d, g, tm=1024):
    M, D = x2d.shape
    return pl.pallas_call(
        _rmsnorm_kernel,
        grid=(M // tm,),
        in_specs=[pl.BlockSpec((tm, D), lambda i: (i, 0)),
                  pl.BlockSpec((1, D), lambda i: (0, 0))],
        out_specs=pl.BlockSpec((tm, D), lambda i: (i, 0)),
        out_shape=jax.ShapeDtypeStruct((M, D), BF16),
        compiler_params=_params(("arbitrary",)),
        name="prenorm",
    )(x2d, g.reshape(1, D))


def _inproj_a_kernel(h_ref, wa_ref, wb_ref, wz_ref, u_ref, sz_ref, wa_s, wb_s, wz_s):
    @pl.when(pl.program_id(1) == 0)
    def _():
        wa_s[...] = wa_ref[...].astype(BF16)
        wb_s[...] = wb_ref[...].astype(BF16)
        wz_s[...] = wz_ref[...].astype(BF16)

    h = h_ref[...]
    a = jnp.dot(h, wa_s[...], preferred_element_type=F32)
    b = jnp.dot(h, wb_s[...], preferred_element_type=F32)
    u_ref[...] = a * _sigmoid(b)
    z = jnp.dot(h, wz_s[...], preferred_element_type=F32)
    sz_ref[...] = z * _sigmoid(z)


def _inproj_a(h, w_in, tm=1024, tn=512):
    M, D = h.shape
    E = w_in.shape[1] // 3
    nb = E // tn
    return pl.pallas_call(
        _inproj_a_kernel,
        grid=(nb, M // tm),
        in_specs=[pl.BlockSpec((tm, D), lambda n, m: (m, 0)),
                  pl.BlockSpec((D, tn), lambda n, m: (0, n)),
                  pl.BlockSpec((D, tn), lambda n, m: (0, n + nb)),
                  pl.BlockSpec((D, tn), lambda n, m: (0, n + 2 * nb))],
        out_specs=[pl.BlockSpec((tm, tn), lambda n, m: (m, n)),
                   pl.BlockSpec((tm, tn), lambda n, m: (m, n))],
        out_shape=[jax.ShapeDtypeStruct((M, E), F32),
                   jax.ShapeDtypeStruct((M, E), F32)],
        scratch_shapes=[pltpu.VMEM((D, tn), BF16)] * 3,
        compiler_params=_params(("arbitrary", "arbitrary")),
        name="inproj_a",
    )(h, w_in, w_in, w_in)


def _conv_out_kernel(ucur_ref, uprev_ref, sz_ref, x_ref, cw_ref, cb_ref, lg_ref, lb_ref,
                     wo_ref, gkv_ref, gb_ref, x1_ref, hk_ref, hb_ref, seg_s, c_s, wo_s, *, tm, pitch, cpitch, oc):
    E = ucur_ref.shape[-1]
    A = tm // SUBLANES
    first = pl.program_id(1) == 0
    base = HALO - (CONV_WIDTH - 1)

    @pl.when((pl.program_id(0) == 0) & first)
    def _():
        wo_s[...] = wo_ref[...].astype(BF16)

    for ct in range(E // LANES):
        cs = slice(ct * LANES, (ct + 1) * LANES)
        seg_s[ct, 0:HALO, :] = jnp.where(first, 0.0, uprev_ref[0, :, cs])
        seg_s[ct, HALO:HALO + A, :] = ucur_ref[0, 0:A, cs]
        for i in range(1, SUBLANES):
            seg_s[ct, i * pitch:i * pitch + HALO + A, :] = ucur_ref[0, i * A - HALO:(i + 1) * A, cs]
        wv = [cw_ref[ct, pl.ds(j, SUBLANES, stride=0), :] for j in range(CONV_WIDTH)]
        bias = cb_ref[ct, pl.ds(0, SUBLANES, stride=0), :]
        for a0 in range(0, A, oc):
            accs = [None] * oc
            for k in range(oc + CONV_WIDTH - 1):
                w = seg_s[ct, pl.ds(a0 + base + k, SUBLANES, stride=pitch), :]
                for o in range(oc):
                    j = k - o
                    if 0 <= j < CONV_WIDTH:
                        t = w * wv[j]
                        accs[o] = t if accs[o] is None else accs[o] + t
            for o in range(oc):
                c_s[ct, pl.ds(a0 + o, SUBLANES, stride=cpitch), :] = accs[o] + bias

    c = jnp.concatenate(
        [jnp.concatenate([c_s[ct, i * cpitch:i * cpitch + A, :] for i in range(SUBLANES)], axis=0)
         for ct in range(E // LANES)], axis=-1)
    mu = jnp.mean(c, axis=-1, keepdims=True)
    cc = c - mu
    var = jnp.mean(cc * cc, axis=-1, keepdims=True)
    cn = cc * lax.rsqrt(var + EPS) * lg_ref[...] + lb_ref[...]
    y = (cn * _sigmoid(cn)) * sz_ref[0]
    o = jnp.dot(y.astype(BF16), wo_s[...], preferred_element_type=F32)
    x1 = x_ref[0] + o
    x1_ref[0] = x1
    xn = x1 * lax.rsqrt(jnp.mean(x1 * x1, axis=-1, keepdims=True) + EPS)
    hk_ref[0] = (xn * gkv_ref[...]).astype(BF16)
    hb_ref[0] = (xn * gb_ref[...]).astype(BF16)


def _conv_out(u, sz, x, conv_w, conv_b, ln_g, ln_b, w_out, g_kv, g_b, tm=256, oc=8):
    B, S, E = u.shape
    D = x.shape[-1]
    nct = E // LANES
    pitch = HALO + tm // SUBLANES + SUBLANES
    cpitch = tm // SUBLANES + SUBLANES
    row = lambda a: a.reshape(1, -1)
    full = lambda shape: pl.BlockSpec(shape, lambda b, i: (0,) * len(shape))
    tile = lambda w: pl.BlockSpec((1, tm, w), lambda b, i: (b, i, 0))
    hpb = tm // HALO
    return pl.pallas_call(
        functools.partial(_conv_out_kernel, tm=tm, pitch=pitch, cpitch=cpitch, oc=oc),
        grid=(B, S // tm),
        in_specs=[tile(E),
                  pl.BlockSpec((1, HALO, E), lambda b, i: (b, jnp.maximum(i * hpb - 1, 0), 0)),
                  tile(E), tile(D),
                  full((nct, CONV_WIDTH, LANES)), full((nct, 1, LANES)), full((1, E)), full((1, E)),
                  full((E, D)), full((1, D)), full((1, D))],
        out_specs=[tile(D), tile(D), tile(D)],
        out_shape=[jax.ShapeDtypeStruct((B, S, D), F32),
                   jax.ShapeDtypeStruct((B, S, D), BF16),
                   jax.ShapeDtypeStruct((B, S, D), BF16)],
        scratch_shapes=[pltpu.VMEM((nct, SUBLANES * pitch, LANES), F32),
                        pltpu.VMEM((nct, SUBLANES * cpitch, LANES), F32),
                        pltpu.VMEM((E, D), BF16)],
        compiler_params=_params(("arbitrary", "arbitrary")),
        name="conv_out",
    )(u, u, sz, x, conv_w.reshape(CONV_WIDTH, nct, LANES).transpose(1, 0, 2),
      conv_b.reshape(nct, 1, LANES), row(ln_g), row(ln_b), w_out, row(g_kv), row(g_b))


def _row_pitch(dil):
    return dil + SUBLANES if dil % (2 * SUBLANES) == 0 else dil


def _proj_kernel(x_ref, w_ref, o_ref, w_s, acc_s, *, dil, epilogue, scale):
    first = (pl.program_id(1) == 0) & (pl.program_id(2) == 0)

    @pl.when(first)
    def _():
        w_s[...] = w_ref[...].astype(BF16)

    acc = jnp.dot(x_ref[0], w_s[...], preferred_element_type=F32)
    if epilogue == "silu":
        acc = acc * _sigmoid(acc)
    elif epilogue == "scale":
        acc = acc * scale
    nh = o_ref.shape[1]
    tl = o_ref.shape[3]
    if dil == 1:
        for hh in range(nh):
            o_ref[0, hh, 0] = acc[:, hh * HEAD_DIM:(hh + 1) * HEAD_DIM].astype(o_ref.dtype)
    else:
        pitch = _row_pitch(dil)
        for hh in range(nh):
            piece = acc[:, hh * HEAD_DIM:(hh + 1) * HEAD_DIM]
            if pitch == dil:
                acc_s[hh] = piece
            else:
                for l in range(tl):
                    acc_s[hh, l * pitch:l * pitch + dil, :] = piece[l * dil:(l + 1) * dil, :]
        for hh in range(nh):
            for r in range(dil):
                o_ref[0, hh, r] = acc_s[hh, pl.ds(r, tl, stride=pitch), :].astype(o_ref.dtype)


def _proj(act, w, col_off, n_cols, dil, out_dtype, epilogue="none", scale=1.0, tn=1024, tm=1024):
    B, S, D = act.shape
    L = S // dil
    tl = tm // dil
    nh = tn // HEAD_DIM
    cb0 = col_off // tn
    out = pl.pallas_call(
        functools.partial(_proj_kernel, dil=dil, epilogue=epilogue, scale=scale),
        grid=(n_cols // tn, B, S // tm),
        in_specs=[pl.BlockSpec((1, tm, D), lambda n, b, i: (b, i, 0)),
                  pl.BlockSpec((D, tn), lambda n, b, i: (0, cb0 + n))],
        out_specs=pl.BlockSpec((1, nh, dil, tl, HEAD_DIM), lambda n, b, i: (b, n, 0, i, 0)),
        out_shape=jax.ShapeDtypeStruct((B, n_cols // HEAD_DIM, dil, L, HEAD_DIM), out_dtype),
        scratch_shapes=[pltpu.VMEM((D, tn), BF16),
                        pltpu.VMEM((nh, tl * _row_pitch(dil), HEAD_DIM), F32)],
        compiler_params=_params(("arbitrary",) * 3),
        name="proj_d%d_%s" % (dil, epilogue),
    )(act, w)
    return out.reshape(B, n_cols // HEAD_DIM, S, HEAD_DIM)


def _bucket_tables():
    delta = (jnp.arange(BLOCK)[:, None] + BLOCK) - jnp.arange(2 * BLOCK)[None, :]
    tabs = []
    for window, dil in DILATED_GROUPS:
        w_sub = window // dil
        local = (delta >= 0) & (delta <= w_sub)
        dist = jnp.clip(delta, 0) * dil
        large = MAX_EXACT + (jnp.log(jnp.maximum(dist, 1).astype(F32) / MAX_EXACT)
                             / math.log(MAX_DISTANCE / MAX_EXACT)
                             * (N_BUCKETS - MAX_EXACT)).astype(jnp.int32)
        large = jnp.minimum(large, N_BUCKETS - 1)
        bucket = jnp.where(dist < MAX_EXACT, dist, large)
        tabs.append(jnp.where(local, bucket, -1).astype(jnp.int32))
    return jnp.stack(tabs)


def _attn_kernel(rb_ref, bkt_ref, q1, q2, q3, k1, k2, k3, v1, v2, v3, sz_ref, y_ref,
                 tbl_s, o_s, m_s, d_s, e_s, *, S, mc, unroll, tile):
    h = pl.program_id(0)

    @pl.when(pl.program_id(1) == 0)
    def _():
        for g in range(N_GROUPS):
            bk = bkt_ref[g]
            t = jnp.full(bk.shape, -jnp.inf, F32)
            for kk in range(N_BUCKETS):
                t = jnp.where(bk == kk, rb_ref[kk * N_HEADS + h] * LOG2E, t)
            tbl_s[g] = t

    groups = ((q1, k1, v1), (q2, k2, v2), (q3, k3, v3))
    bpt = tile // BLOCK

    blocks = [(g, j) for g in range(N_GROUPS) for j in range(bpt)]
    n_sets = len(blocks) // unroll

    def tile_body(T):
        def geometry(g, j):
            dil = DILATED_GROUPS[g][1]
            nb = S // BLOCK // dil
            bpr = bpt // dil
            r, jl = j // bpr, j % bpr
            lb = T * bpr + jl
            p = r * nb + lb
            nk = 1 if lb == 0 else 2
            row = p * BLOCK
            krow = (p + 1 - nk) * BLOCK
            pitch = _row_pitch(dil)
            t0 = jl * (BLOCK * pitch) + r
            idx = pl.ds(t0, BLOCK) if dil == 1 else pl.ds(t0, BLOCK, stride=pitch)
            return row, krow, nk, idx

        def scores(g, j, slot):
            row, krow, nk, idx = geometry(g, j)
            q = groups[g][0][0, 0, pl.ds(row, BLOCK), :]
            k = groups[g][1][0, 0, pl.ds(krow, nk * BLOCK), :]
            s = lax.dot_general(q, k, (((1,), (1,)), ((), ())), preferred_element_type=F32)
            s = s + tbl_s[g, :, (2 - nk) * BLOCK:]
            m = jnp.max(s, axis=-1, keepdims=True)
            e_s[slot, :, 0:nk * BLOCK] = jnp.exp2(s - m).astype(BF16)
            m_s[g, idx, :] = jnp.broadcast_to(m, (BLOCK, HEAD_DIM))

        def values(g, j, slot):
            row, krow, nk, idx = geometry(g, j)
            v = groups[g][2][0, 0, pl.ds(krow, nk * BLOCK), :]
            va = jnp.concatenate([v, jnp.ones_like(v)], axis=1)
            acc = jnp.dot(e_s[slot, :, 0:nk * BLOCK], va, preferred_element_type=F32)
            o_s[g, idx, :] = acc[:, :HEAD_DIM]
            d_s[g, idx, :] = acc[:, HEAD_DIM:]

        for k in range(n_sets + 1):
            for uu in range(unroll):
                if k < n_sets:
                    scores(*blocks[k * unroll + uu], (k % 2) * unroll + uu)
                if k > 0:
                    values(*blocks[(k - 1) * unroll + uu], ((k - 1) % 2) * unroll + uu)

        def merge(c, c2):
            rows = pl.ds(pl.multiple_of(c * mc, mc), mc)
            def staged(ref, g):
                dil = DILATED_GROUPS[g][1]
                pitch = _row_pitch(dil)
                if pitch == dil:
                    return ref[g, rows, :]
                return jnp.concatenate(
                    [ref[g, pl.ds(pl.multiple_of((c * (mc // dil) + i) * pitch, SUBLANES), dil), :]
                     for i in range(mc // dil)], axis=0)

            m1, m2, m3 = staged(m_s, 0), staged(m_s, 1), staged(m_s, 2)
            mx = jnp.maximum(jnp.maximum(m1, m2), m3)
            w1, w2, w3 = jnp.exp2(m1 - mx), jnp.exp2(m2 - mx), jnp.exp2(m3 - mx)
            num = w1 * staged(o_s, 0) + w2 * staged(o_s, 1) + w3 * staged(o_s, 2)
            den = w1 * staged(d_s, 0) + w2 * staged(d_s, 1) + w3 * staged(d_s, 2)
            orow = pl.ds(pl.multiple_of(T * tile + c * mc, mc), mc)
            y_ref[0, 0, orow, :] = ((num / den) * sz_ref[0, 0, orow, :]).astype(y_ref.dtype)
            return c2

        lax.fori_loop(0, tile // mc, merge, 0)

    for T in range(S // tile):
        tile_body(T)


def _attention(rel_bias, qs, kvs, sz, mc=256, unroll=4, tile=2048):
    B, H, S, Dh = qs[0].shape
    bkt = _bucket_tables()
    rows = max(tile // dil * _row_pitch(dil) for _, dil in DILATED_GROUPS)
    hm = lambda off: pl.BlockSpec((1, 1, S, Dh), lambda h, b: (b, h + off, 0, 0))
    in_specs = ([pl.BlockSpec(memory_space=pltpu.SMEM),
                 pl.BlockSpec(bkt.shape, lambda h, b: (0, 0, 0))]
                + [hm(0)] * 3 + [hm(0)] * 3 + [hm(H)] * 3 + [hm(0)])
    return pl.pallas_call(
        functools.partial(_attn_kernel, S=S, mc=mc, unroll=unroll, tile=tile),
        grid=(H, B),
        in_specs=in_specs,
        out_specs=hm(0),
        out_shape=jax.ShapeDtypeStruct((B, H, S, Dh), BF16),
        scratch_shapes=[pltpu.VMEM((N_GROUPS, BLOCK, 2 * BLOCK), F32),
                        pltpu.VMEM((N_GROUPS, rows, Dh), F32),
                        pltpu.VMEM((N_GROUPS, rows, Dh), F32),
                        pltpu.VMEM((N_GROUPS, rows, Dh), F32),
                        pltpu.VMEM((2 * unroll, BLOCK, 2 * BLOCK), BF16)],
        compiler_params=_params(("arbitrary", "arbitrary")),
        name="attention",
    )(rel_bias.reshape(-1), bkt, *qs, *kvs, *kvs, sz)


def _final_kernel(y_ref, w_ref, x1_ref, g_ref, o_ref, w_s):
    @pl.when((pl.program_id(0) == 0) & (pl.program_id(1) == 0))
    def _():
        w_s[...] = w_ref[...].astype(BF16)

    y = jnp.concatenate([y_ref[0, h] for h in range(y_ref.shape[1])], axis=-1)
    x2 = x1_ref[0] + jnp.dot(y, w_s[...], preferred_element_type=F32)
    ms = jnp.mean(x2 * x2, axis=-1, keepdims=True)
    o_ref[0] = x2 * lax.rsqrt(ms + EPS) * g_ref[...]


def _final(y, w, x1, g, tm=512):
    B, H, S, Dh = y.shape
    D = w.shape[1]
    return pl.pallas_call(
        _final_kernel,
        grid=(B, S // tm),
        in_specs=[pl.BlockSpec((1, H, tm, Dh), lambda b, i: (b, 0, i, 0)),
                  pl.BlockSpec((H * Dh, D), lambda b, i: (0, 0)),
                  pl.BlockSpec((1, tm, D), lambda b, i: (b, i, 0)),
                  pl.BlockSpec((1, D), lambda b, i: (0, 0))],
        out_specs=pl.BlockSpec((1, tm, D), lambda b, i: (b, i, 0)),
        out_shape=jax.ShapeDtypeStruct((B, S, D), F32),
        scratch_shapes=[pltpu.VMEM((H * Dh, D), BF16)],
        compiler_params=_params(("arbitrary", "arbitrary")),
        name="outproj_final",
    )(y, w, x1, g.reshape(1, D))


def kernel(x, a_norm, a_w_in, a_conv_w, a_conv_b, a_ln_g, a_ln_b, a_w_out, kv_norm, w_kv,
           b_norm, b_w_in, b_w_out, rel_bias, final_norm):
    B, S, D = x.shape
    M = B * S
    h0 = _rmsnorm_bf16(x.reshape(M, D), a_norm[0])
    u, sz0 = _inproj_a(h0, a_w_in[0])
    E = u.shape[-1]
    x1, hk, hb = _conv_out(u.reshape(B, S, E), sz0.reshape(B, S, E), x, a_conv_w[0], a_conv_b[0],
                           a_ln_g[0], a_ln_b[0], a_w_out[0], kv_norm, b_norm[0])
    A = N_HEADS * HEAD_DIM
    qs, kvs = [], []
    for g, (_, dil) in enumerate(DILATED_GROUPS):
        qs.append(_proj(hb, b_w_in[0], g * A, A, dil, BF16, "scale", HEAD_DIM ** -0.5 * LOG2E))
        kvs.append(_proj(hk, w_kv, 2 * g * A, 2 * A, dil, BF16))
    sz1 = _proj(hb, b_w_in[0], N_GROUPS * A, A, 1, F32, "silu")
    y = _attention(rel_bias, qs, kvs, sz1)
    return _final(y, b_w_out[0], x1, final_norm)
```

```python
import functools
import math

import jax
import jax.numpy as jnp
from jax import lax
from jax.experimental import pallas as pl
from jax.experimental.pallas import tpu as pltpu

D_MODEL = 2048
CONV_WIDTH = 31
HEAD_DIM = 128
N_HEADS = D_MODEL // HEAD_DIM
DILATED_GROUPS = ((128, 1), (512, 4), (2048, 16))
N_GROUPS = len(DILATED_GROUPS)
BLOCK = 128
N_BUCKETS = 32
MAX_EXACT = N_BUCKETS // 2
MAX_DISTANCE = 2048
EPS = 1e-6

LANES = 128
SUBLANES = 8
HALO = 32

F32 = jnp.float32
BF16 = jnp.bfloat16
VMEM_LIMIT = 56 * 1024 * 1024
LOG2E = math.log2(math.e)


def _sigmoid(x):
    return 0.5 * jnp.tanh(0.5 * x) + 0.5


def _params(sem):
    return pltpu.CompilerParams(dimension_semantics=sem, vmem_limit_bytes=VMEM_LIMIT)


def _rmsnorm_kernel(x_ref, g_ref, o_ref):
    x = x_ref[...]
    ms = jnp.mean(x * x, axis=-1, keepdims=True)
    o_ref[...] = (x * lax.rsqrt(ms + EPS) * g_ref[...]).astype(o_ref.dtype)


def _rmsnorm_bf16(x2d, g, tm=1024):
    M, D = x2d.shape
    return pl.pallas_call(
        _rmsnorm_kernel,
        grid=(M // tm,),
        in_specs=[pl.BlockSpec((tm, D), lambda i: (i, 0)),
                  pl.BlockSpec((1, D), lambda i: (0, 0))],
        out_specs=pl.BlockSpec((tm, D), lambda i: (i, 0)),
        out_shape=jax.ShapeDtypeStruct((M, D), BF16),
        compiler_params=_params(("arbitrary",)),
        name="prenorm",
    )(x2d, g.reshape(1, D))


def _inproj_a_kernel(h_ref, wa_ref, wb_ref, wz_ref, u_ref, sz_ref, wa_s, wb_s, wz_s):
    @pl.when(pl.program_id(1) == 0)
    def _():
        wa_s[...] = wa_ref[...].astype(BF16)
        wb_s[...] = wb_ref[...].astype(BF16)
        wz_s[...] = wz_ref[...].astype(BF16)

    h = h_ref[...]
    a = jnp.dot(h, wa_s[...], preferred_element_type=F32)
    b = jnp.dot(h, wb_s[...], preferred_element_type=F32)
    u_ref[...] = a * _sigmoid(b)
    z = jnp.dot(h, wz_s[...], preferred_element_type=F32)
    sz_ref[...] = z * _sigmoid(z)


def _inproj_a(h, w_in, tm=1024, tn=512):
    M, D = h.shape
    E = w_in.shape[1] // 3
    nb = E // tn
    return pl.pallas_call(
        _inproj_a_kernel,
        grid=(nb, M // tm),
        in_specs=[pl.BlockSpec((tm, D), lambda n, m: (m, 0)),
                  pl.BlockSpec((D, tn), lambda n, m: (0, n)),
                  pl.BlockSpec((D, tn), lambda n, m: (0, n + nb)),
                  pl.BlockSpec((D, tn), lambda n, m: (0, n + 2 * nb))],
        out_specs=[pl.BlockSpec((tm, tn), lambda n, m: (m, n)),
                   pl.BlockSpec((tm, tn), lambda n, m: (m, n))],
        out_shape=[jax.ShapeDtypeStruct((M, E), F32),
                   jax.ShapeDtypeStruct((M, E), F32)],
        scratch_shapes=[pltpu.VMEM((D, tn), BF16)] * 3,
        compiler_params=_params(("arbitrary", "arbitrary")),
        name="inproj_a",
    )(h, w_in, w_in, w_in)


def _conv_out_kernel(ucur_ref, uprev_ref, sz_ref, x_ref, cw_ref, cb_ref, lg_ref, lb_ref,
                     wo_ref, gkv_ref, gb_ref, x1_ref, hk_ref, hb_ref, seg_s, c_s, wo_s, *, tm, pitch, cpitch, oc):
    E = ucur_ref.shape[-1]
    A = tm // SUBLANES
    first = pl.program_id(1) == 0
    base = HALO - (CONV_WIDTH - 1)

    @pl.when((pl.program_id(0) == 0) & first)
    def _():
        wo_s[...] = wo_ref[...].astype(BF16)

    for ct in range(E // LANES):
        cs = slice(ct * LANES, (ct + 1) * LANES)
        seg_s[ct, 0:HALO, :] = jnp.where(first, 0.0, uprev_ref[0, :, cs])
        seg_s[ct, HALO:HALO + A, :] = ucur_ref[0, 0:A, cs]
        for i in range(1, SUBLANES):
            seg_s[ct, i * pitch:i * pitch + HALO + A, :] = ucur_ref[0, i * A - HALO:(i + 1) * A, cs]
        wv = [cw_ref[ct, pl.ds(j, SUBLANES, stride=0), :] for j in range(CONV_WIDTH)]
        bias = cb_ref[ct, pl.ds(0, SUBLANES, stride=0), :]
        for a0 in range(0, A, oc):
            accs = [None] * oc
            for k in range(oc + CONV_WIDTH - 1):
                w = seg_s[ct, pl.ds(a0 + base + k, SUBLANES, stride=pitch), :]
                for o in range(oc):
                    j = k - o
                    if 0 <= j < CONV_WIDTH:
                        t = w * wv[j]
                        accs[o] = t if accs[o] is None else accs[o] + t
            for o in range(oc):
                c_s[ct, pl.ds(a0 + o, SUBLANES, stride=cpitch), :] = accs[o] + bias

    c = jnp.concatenate(
        [jnp.concatenate([c_s[ct, i * cpitch:i * cpitch + A, :] for i in range(SUBLANES)], axis=0)
         for ct in range(E // LANES)], axis=-1)
    mu = jnp.mean(c, axis=-1, keepdims=True)
    cc = c - mu
    var = jnp.mean(cc * cc, axis=-1, keepdims=True)
    cn = cc * lax.rsqrt(var + EPS) * lg_ref[...] + lb_ref[...]
    y = (cn * _sigmoid(cn)) * sz_ref[0]
    o = jnp.dot(y.astype(BF16), wo_s[...], preferred_element_type=F32)
    x1 = x_ref[0] + o
    x1_ref[0] = x1
    xn = x1 * lax.rsqrt(jnp.mean(x1 * x1, axis=-1, keepdims=True) + EPS)
    hk_ref[0] = (xn * gkv_ref[...]).astype(BF16)
    hb_ref[0] = (xn * gb_ref[...]).astype(BF16)


def _conv_out(u, sz, x, conv_w, conv_b, ln_g, ln_b, w_out, g_kv, g_b, tm=256, oc=8):
    B, S, E = u.shape
    D = x.shape[-1]
    nct = E // LANES
    pitch = HALO + tm // SUBLANES + SUBLANES
    cpitch = tm // SUBLANES + SUBLANES
    row = lambda a: a.reshape(1, -1)
    full = lambda shape: pl.BlockSpec(shape, lambda b, i: (0,) * len(shape))
    tile = lambda w: pl.BlockSpec((1, tm, w), lambda b, i: (b, i, 0))
    hpb = tm // HALO
    return pl.pallas_call(
        functools.partial(_conv_out_kernel, tm=tm, pitch=pitch, cpitch=cpitch, oc=oc),
        grid=(B, S // tm),
        in_specs=[tile(E),
                  pl.BlockSpec((1, HALO, E), lambda b, i: (b, jnp.maximum(i * hpb - 1, 0), 0)),
                  tile(E), tile(D),
                  full((nct, CONV_WIDTH, LANES)), full((nct, 1, LANES)), full((1, E)), full((1, E)),
                  full((E, D)), full((1, D)), full((1, D))],
        out_specs=[tile(D), tile(D), tile(D)],
        out_shape=[jax.ShapeDtypeStruct((B, S, D), F32),
                   jax.ShapeDtypeStruct((B, S, D), BF16),
                   jax.ShapeDtypeStruct((B, S, D), BF16)],
        scratch_shapes=[pltpu.VMEM((nct, SUBLANES * pitch, LANES), F32),
                        pltpu.VMEM((nct, SUBLANES * cpitch, LANES), F32),
                        pltpu.VMEM((E, D), BF16)],
        compiler_params=_params(("arbitrary", "arbitrary")),
        name="conv_out",
    )(u, u, sz, x, conv_w.reshape(CONV_WIDTH, nct, LANES).transpose(1, 0, 2),
      conv_b.reshape(nct, 1, LANES), row(ln_g), row(ln_b), w_out, row(g_kv), row(g_b))


def _row_pitch(dil):
    return dil + SUBLANES if dil % (2 * SUBLANES) == 0 else dil


def _proj_kernel(x_ref, w_ref, o_ref, w_s, acc_s, *, dil, epilogue, scale):
    first = (pl.program_id(1) == 0) & (pl.program_id(2) == 0)

    @pl.when(first)
    def _():
        w_s[...] = w_ref[...].astype(BF16)

    acc = jnp.dot(x_ref[0], w_s[...], preferred_element_type=F32)
    if epilogue == "silu":
        acc = acc * _sigmoid(acc)
    elif epilogue == "scale":
        acc = acc * scale
    nh = o_ref.shape[1]
    tl = o_ref.shape[3]
    if dil == 1:
        for hh in range(nh):
            o_ref[0, hh, 0] = acc[:, hh * HEAD_DIM:(hh + 1) * HEAD_DIM].astype(o_ref.dtype)
    else:
        pitch = _row_pitch(dil)
        for hh in range(nh):
            piece = acc[:, hh * HEAD_DIM:(hh + 1) * HEAD_DIM]
            if pitch == dil:
                acc_s[hh] = piece
            else:
                for l in range(tl):
                    acc_s[hh, l * pitch:l * pitch + dil, :] = piece[l * dil:(l + 1) * dil, :]
        for hh in range(nh):
            for r in range(dil):
                o_ref[0, hh, r] = acc_s[hh, pl.ds(r, tl, stride=pitch), :].astype(o_ref.dtype)


def _proj(act, w, col_off, n_cols, dil, out_dtype, epilogue="none", scale=1.0, tn=1024, tm=1024):
    B, S, D = act.shape
    L = S // dil
    tl = tm // dil
    nh = tn // HEAD_DIM
    cb0 = col_off // tn
    out = pl.pallas_call(
        functools.partial(_proj_kernel, dil=dil, epilogue=epilogue, scale=scale),
        grid=(n_cols // tn, B, S // tm),
        in_specs=[pl.BlockSpec((1, tm, D), lambda n, b, i: (b, i, 0)),
                  pl.BlockSpec((D, tn), lambda n, b, i: (0, cb0 + n))],
        out_specs=pl.BlockSpec((1, nh, dil, tl, HEAD_DIM), lambda n, b, i: (b, n, 0, i, 0)),
        out_shape=jax.ShapeDtypeStruct((B, n_cols // HEAD_DIM, dil, L, HEAD_DIM), out_dtype),
        scratch_shapes=[pltpu.VMEM((D, tn), BF16),
                        pltpu.VMEM((nh, tl * _row_pitch(dil), HEAD_DIM), F32)],
        compiler_params=_params(("arbitrary",) * 3),
        name="proj_d%d_%s" % (dil, epilogue),
    )(act, w)
    return out.reshape(B, n_cols // HEAD_DIM, S, HEAD_DIM)


def _bucket_rows():
    delta = BLOCK - jnp.arange(2 * BLOCK)
    rows = []
    for window, dil in DILATED_GROUPS:
        assert window // dil == BLOCK
        dist = jnp.clip(delta, 0) * dil
        large = MAX_EXACT + (jnp.log(jnp.maximum(dist, 1).astype(F32) / MAX_EXACT)
                             / math.log(MAX_DISTANCE / MAX_EXACT)
                             * (N_BUCKETS - MAX_EXACT)).astype(jnp.int32)
        large = jnp.minimum(large, N_BUCKETS - 1)
        bucket = jnp.where(dist < MAX_EXACT, dist, large)
        rows.append(jnp.where(delta >= 0, bucket, -1).astype(jnp.int32))
    return jnp.broadcast_to(jnp.stack(rows)[:, None, :], (N_GROUPS, SUBLANES, 2 * BLOCK))


def _attn_kernel(rb_ref, bkt_ref, q1, q2, q3, k1, k2, k3, v1, v2, v3, sz_ref, y_ref,
                 tbl_s, o_s, m_s, d_s, e_s, *, S, mc, unroll, tile):
    h = pl.program_id(0)

    @pl.when(pl.program_id(1) == 0)
    def _():
        for g in range(N_GROUPS):
            bk = bkt_ref[g]
            row = jnp.full(bk.shape, -jnp.inf, F32)
            for kk in range(N_BUCKETS):
                row = jnp.where(bk == kk, rb_ref[kk * N_HEADS + h] * LOG2E, row)
            tbl_s[g] = pltpu.roll(jnp.broadcast_to(row[0:1, :], (BLOCK, 2 * BLOCK)), 0, 1,
                                  stride=1, stride_axis=0)

    groups = ((q1, k1, v1), (q2, k2, v2), (q3, k3, v3))
    bpt = tile // BLOCK

    blocks = [(g, j) for g in range(N_GROUPS) for j in range(bpt)]
    n_sets = len(blocks) // unroll

    def tile_body(T):
        def geometry(g, j):
            dil = DILATED_GROUPS[g][1]
            nb = S // BLOCK // dil
            bpr = bpt // dil
            r, jl = j // bpr, j % bpr
            lb = T * bpr + jl
            p = r * nb + lb
            nk = 1 if lb == 0 else 2
            row = p * BLOCK
            krow = (p + 1 - nk) * BLOCK
            pitch = _row_pitch(dil)
            t0 = jl * (BLOCK * pitch) + r
            idx = pl.ds(t0, BLOCK) if dil == 1 else pl.ds(t0, BLOCK, stride=pitch)
            return row, krow, nk, idx

        def scores(g, j, slot):
            row, krow, nk, idx = geometry(g, j)
            q = groups[g][0][0, 0, pl.ds(row, BLOCK), :]
            k = groups[g][1][0, 0, pl.ds(krow, nk * BLOCK), :]
            s = lax.dot_general(q, k, (((1,), (1,)), ((), ())), preferred_element_type=F32)
            s = s + tbl_s[g, :, (2 - nk) * BLOCK:]
            m = jnp.max(s, axis=-1, keepdims=True)
            e_s[slot, :, 0:nk * BLOCK] = jnp.exp2(s - m).astype(BF16)
            m_s[g, idx, :] = jnp.broadcast_to(m, (BLOCK, HEAD_DIM))

        def values(g, j, slot):
            row, krow, nk, idx = geometry(g, j)
            v = groups[g][2][0, 0, pl.ds(krow, nk * BLOCK), :]
            va = jnp.concatenate([v, jnp.ones_like(v)], axis=1)
            acc = jnp.dot(e_s[slot, :, 0:nk * BLOCK], va, preferred_element_type=F32)
            o_s[g, idx, :] = acc[:, :HEAD_DIM]
            d_s[g, idx, :] = acc[:, HEAD_DIM:]

        for k in range(n_sets + 1):
            for uu in range(unroll):
                if k < n_sets:
                    scores(*blocks[k * unroll + uu], (k % 2) * unroll + uu)
                if k > 0:
                    values(*blocks[(k - 1) * unroll + uu], ((k - 1) % 2) * unroll + uu)

        def merge(c, c2):
            rows = pl.ds(pl.multiple_of(c * mc, mc), mc)
            def staged(ref, g):
                dil = DILATED_GROUPS[g][1]
                pitch = _row_pitch(dil)
                if pitch == dil:
                    return ref[g, rows, :]
                return jnp.concatenate(
                    [ref[g, pl.ds(pl.multiple_of((c * (mc // dil) + i) * pitch, SUBLANES), dil), :]
                     for i in range(mc // dil)], axis=0)

            m1, m2, m3 = staged(m_s, 0), staged(m_s, 1), staged(m_s, 2)
            mx = jnp.maximum(jnp.maximum(m1, m2), m3)
            w1, w2, w3 = jnp.exp2(m1 - mx), jnp.exp2(m2 - mx), jnp.exp2(m3 - mx)
            num = w1 * staged(o_s, 0) + w2 * staged(o_s, 1) + w3 * staged(o_s, 2)
            den = w1 * staged(d_s, 0) + w2 * staged(d_s, 1) + w3 * staged(d_s, 2)
            orow = pl.ds(pl.multiple_of(T * tile + c * mc, mc), mc)
            y_ref[0, 0, orow, :] = ((num / den) * sz_ref[0, 0, orow, :]).astype(y_ref.dtype)
            return c2

        lax.fori_loop(0, tile // mc, merge, 0)

    for T in range(S // tile):
        tile_body(T)


def _attention(rel_bias, qs, kvs, sz, mc=256, unroll=4, tile=2048):
    B, H, S, Dh = qs[0].shape
    bkt = _bucket_rows()
    rows = max(tile // dil * _row_pitch(dil) for _, dil in DILATED_GROUPS)
    hm = lambda off: pl.BlockSpec((1, 1, S, Dh), lambda h, b: (b, h + off, 0, 0))
    in_specs = ([pl.BlockSpec(memory_space=pltpu.SMEM),
                 pl.BlockSpec(bkt.shape, lambda h, b: (0, 0, 0))]
                + [hm(0)] * 3 + [hm(0)] * 3 + [hm(H)] * 3 + [hm(0)])
    return pl.pallas_call(
        functools.partial(_attn_kernel, S=S, mc=mc, unroll=unroll, tile=tile),
        grid=(H, B),
        in_specs=in_specs,
        out_specs=hm(0),
        out_shape=jax.ShapeDtypeStruct((B, H, S, Dh), BF16),
        scratch_shapes=[pltpu.VMEM((N_GROUPS, BLOCK, 2 * BLOCK), F32),
                        pltpu.VMEM((N_GROUPS, rows, Dh), F32),
                        pltpu.VMEM((N_GROUPS, rows, Dh), F32),
                        pltpu.VMEM((N_GROUPS, rows, Dh), F32),
                        pltpu.VMEM((2 * unroll, BLOCK, 2 * BLOCK), BF16)],
        compiler_params=_params(("arbitrary", "arbitrary")),
        name="attention",
    )(rel_bias.reshape(-1), bkt, *qs, *kvs, *kvs, sz)


def _final_kernel(y_ref, w_ref, x1_ref, g_ref, o_ref, w_s):
    @pl.when((pl.program_id(0) == 0) & (pl.program_id(1) == 0))
    def _():
        w_s[...] = w_ref[...].astype(BF16)

    y = jnp.concatenate([y_ref[0, h] for h in range(y_ref.shape[1])], axis=-1)
    x2 = x1_ref[0] + jnp.dot(y, w_s[...], preferred_element_type=F32)
    ms = jnp.mean(x2 * x2, axis=-1, keepdims=True)
    o_ref[0] = x2 * lax.rsqrt(ms + EPS) * g_ref[...]


def _final(y, w, x1, g, tm=512):
    B, H, S, Dh = y.shape
    D = w.shape[1]
    return pl.pallas_call(
        _final_kernel,
        grid=(B, S // tm),
        in_specs=[pl.BlockSpec((1, H, tm, Dh), lambda b, i: (b, 0, i, 0)),
                  pl.BlockSpec((H * Dh, D), lambda b, i: (0, 0)),
                  pl.BlockSpec((1, tm, D), lambda b, i: (b, i, 0)),
                  pl.BlockSpec((1, D), lambda b, i: (0, 0))],
        out_specs=pl.BlockSpec((1, tm, D), lambda b, i: (b, i, 0)),
        out_shape=jax.ShapeDtypeStruct((B, S, D), F32),
        scratch_shapes=[pltpu.VMEM((H * Dh, D), BF16)],
        compiler_params=_params(("arbitrary", "arbitrary")),
        name="outproj_final",
    )(y, w, x1, g.reshape(1, D))


def kernel(x, a_norm, a_w_in, a_conv_w, a_conv_b, a_ln_g, a_ln_b, a_w_out, kv_norm, w_kv,
           b_norm, b_w_in, b_w_out, rel_bias, final_norm):
    B, S, D = x.shape
    M = B * S
    h0 = _rmsnorm_bf16(x.reshape(M, D), a_norm[0])
    u, sz0 = _inproj_a(h0, a_w_in[0])
    E = u.shape[-1]
    x1, hk, hb = _conv_out(u.reshape(B, S, E), sz0.reshape(B, S, E), x, a_conv_w[0], a_conv_b[0],
                           a_ln_g[0], a_ln_b[0], a_w_out[0], kv_norm, b_norm[0])
    A = N_HEADS * HEAD_DIM
    qs, kvs = [], []
    for g, (_, dil) in enumerate(DILATED_GROUPS):
        tm = 2048 if dil == 1 else 1024
        qs.append(_proj(hb, b_w_in[0], g * A, A, dil, BF16, "scale", HEAD_DIM ** -0.5 * LOG2E, tm=tm))
        kvs.append(_proj(hk, w_kv, 2 * g * A, 2 * A, dil, BF16, tm=tm))
    sz1 = _proj(hb, b_w_in[0], N_GROUPS * A, A, 1, F32, "silu")
    y = _attention(rel_bias, qs, kvs, sz1)
    return _final(y, b_w_out[0], x1, final_norm)
```

```python
import functools
import math

import jax
import jax.numpy as jnp
from jax import lax
from jax.experimental import pallas as pl
from jax.experimental.pallas import tpu as pltpu

D_MODEL = 2048
CONV_WIDTH = 31
HEAD_DIM = 128
N_HEADS = D_MODEL // HEAD_DIM
DILATED_GROUPS = ((128, 1), (512, 4), (2048, 16))
N_GROUPS = len(DILATED_GROUPS)
BLOCK = 128
N_BUCKETS = 32
MAX_EXACT = N_BUCKETS // 2
MAX_DISTANCE = 2048
EPS = 1e-6

LANES = 128
SUBLANES = 8
HALO = 32

F32 = jnp.float32
BF16 = jnp.bfloat16
VMEM_LIMIT = 56 * 1024 * 1024
LOG2E = math.log2(math.e)


def _sigmoid(x):
    return 0.5 * jnp.tanh(0.5 * x) + 0.5


def _params(sem):
    return pltpu.CompilerParams(dimension_semantics=sem, vmem_limit_bytes=VMEM_LIMIT)


def _rmsnorm_kernel(x_ref, g_ref, o_ref):
    x = x_ref[...]
    ms = jnp.mean(x * x, axis=-1, keepdims=True)
    o_ref[...] = (x * lax.rsqrt(ms + EPS) * g_ref[...]).astype(o_ref.dtype)


def _rmsnorm_bf16(x2d, g, tm=1024):
    M, D = x2d.shape
    return pl.pallas_call(
        _rmsnorm_kernel,
        grid=(M // tm,),
        in_specs=[pl.BlockSpec((tm, D), lambda i: (i, 0)),
                  pl.BlockSpec((1, D), lambda i: (0, 0))],
        out_specs=pl.BlockSpec((tm, D), lambda i: (i, 0)),
        out_shape=jax.ShapeDtypeStruct((M, D), BF16),
        compiler_params=_params(("arbitrary",)),
        name="prenorm",
    )(x2d, g.reshape(1, D))


def _inproj_a_kernel(h_ref, wa_ref, wb_ref, wz_ref, u_ref, sz_ref, wa_s, wb_s, wz_s):
    @pl.when(pl.program_id(1) == 0)
    def _():
        wa_s[...] = wa_ref[...].astype(BF16)
        wb_s[...] = wb_ref[...].astype(BF16)
        wz_s[...] = wz_ref[...].astype(BF16)

    h = h_ref[...]
    a = jnp.dot(h, wa_s[...], preferred_element_type=F32)
    b = jnp.dot(h, wb_s[...], preferred_element_type=F32)
    u_ref[...] = a * _sigmoid(b)
    z = jnp.dot(h, wz_s[...], preferred_element_type=F32)
    sz_ref[...] = z * _sigmoid(z)


def _inproj_a(h, w_in, tm=1024, tn=512):
    M, D = h.shape
    E = w_in.shape[1] // 3
    nb = E // tn
    return pl.pallas_call(
        _inproj_a_kernel,
        grid=(nb, M // tm),
        in_specs=[pl.BlockSpec((tm, D), lambda n, m: (m, 0)),
                  pl.BlockSpec((D, tn), lambda n, m: (0, n)),
                  pl.BlockSpec((D, tn), lambda n, m: (0, n + nb)),
                  pl.BlockSpec((D, tn), lambda n, m: (0, n + 2 * nb))],
        out_specs=[pl.BlockSpec((tm, tn), lambda n, m: (m, n)),
                   pl.BlockSpec((tm, tn), lambda n, m: (m, n))],
        out_shape=[jax.ShapeDtypeStruct((M, E), F32),
                   jax.ShapeDtypeStruct((M, E), F32)],
        scratch_shapes=[pltpu.VMEM((D, tn), BF16)] * 3,
        compiler_params=_params(("arbitrary", "arbitrary")),
        name="inproj_a",
    )(h, w_in, w_in, w_in)


def _conv_out_kernel(ucur_ref, uprev_ref, sz_ref, x_ref, cw_ref, cb_ref, lg_ref, lb_ref,
                     wo_ref, gkv_ref, gb_ref, x1_ref, hk_ref, hb_ref, seg_s, c_s, wo_s, *, tm, pitch, cpitch, oc):
    E = ucur_ref.shape[-1]
    A = tm // SUBLANES
    first = pl.program_id(1) == 0
    base = HALO - (CONV_WIDTH - 1)

    @pl.when((pl.program_id(0) == 0) & first)
    def _():
        wo_s[...] = wo_ref[...].astype(BF16)

    for ct in range(E // LANES):
        cs = slice(ct * LANES, (ct + 1) * LANES)
        seg_s[ct, 0:HALO, :] = jnp.where(first, 0.0, uprev_ref[0, :, cs])
        seg_s[ct, HALO:HALO + A, :] = ucur_ref[0, 0:A, cs]
        for i in range(1, SUBLANES):
            seg_s[ct, i * pitch:i * pitch + HALO + A, :] = ucur_ref[0, i * A - HALO:(i + 1) * A, cs]
        wv = [cw_ref[ct, pl.ds(j, SUBLANES, stride=0), :] for j in range(CONV_WIDTH)]
        bias = cb_ref[ct, pl.ds(0, SUBLANES, stride=0), :]
        for a0 in range(0, A, oc):
            accs = [None] * oc
            for k in range(oc + CONV_WIDTH - 1):
                w = seg_s[ct, pl.ds(a0 + base + k, SUBLANES, stride=pitch), :]
                for o in range(oc):
                    j = k - o
                    if 0 <= j < CONV_WIDTH:
                        t = w * wv[j]
                        accs[o] = t if accs[o] is None else accs[o] + t
            for o in range(oc):
                c_s[ct, pl.ds(a0 + o, SUBLANES, stride=cpitch), :] = accs[o] + bias

    c = jnp.concatenate(
        [jnp.concatenate([c_s[ct, i * cpitch:i * cpitch + A, :] for i in range(SUBLANES)], axis=0)
         for ct in range(E // LANES)], axis=-1)
    mu = jnp.mean(c, axis=-1, keepdims=True)
    cc = c - mu
    var = jnp.mean(cc * cc, axis=-1, keepdims=True)
    cn = cc * lax.rsqrt(var + EPS) * lg_ref[...] + lb_ref[...]
    y = (cn * _sigmoid(cn)) * sz_ref[0]
    o = jnp.dot(y.astype(BF16), wo_s[...], preferred_element_type=F32)
    x1 = x_ref[0] + o
    x1_ref[0] = x1
    xn = x1 * lax.rsqrt(jnp.mean(x1 * x1, axis=-1, keepdims=True) + EPS)
    hk_ref[0] = (xn * gkv_ref[...]).astype(BF16)
    hb_ref[0] = (xn * gb_ref[...]).astype(BF16)


def _conv_out(u, sz, x, conv_w, conv_b, ln_g, ln_b, w_out, g_kv, g_b, tm=256, oc=8):
    B, S, E = u.shape
    D = x.shape[-1]
    nct = E // LANES
    pitch = HALO + tm // SUBLANES + SUBLANES
    cpitch = tm // SUBLANES + SUBLANES
    row = lambda a: a.reshape(1, -1)
    full = lambda shape: pl.BlockSpec(shape, lambda b, i: (0,) * len(shape))
    tile = lambda w: pl.BlockSpec((1, tm, w), lambda b, i: (b, i, 0))
    hpb = tm // HALO
    return pl.pallas_call(
        functools.partial(_conv_out_kernel, tm=tm, pitch=pitch, cpitch=cpitch, oc=oc),
        grid=(B, S // tm),
        in_specs=[tile(E),
                  pl.BlockSpec((1, HALO, E), lambda b, i: (b, jnp.maximum(i * hpb - 1, 0), 0)),
                  tile(E), tile(D),
                  full((nct, CONV_WIDTH, LANES)), full((nct, 1, LANES)), full((1, E)), full((1, E)),
                  full((E, D)), full((1, D)), full((1, D))],
        out_specs=[tile(D), tile(D), tile(D)],
        out_shape=[jax.ShapeDtypeStruct((B, S, D), F32),
                   jax.ShapeDtypeStruct((B, S, D), BF16),
                   jax.ShapeDtypeStruct((B, S, D), BF16)],
        scratch_shapes=[pltpu.VMEM((nct, SUBLANES * pitch, LANES), F32),
                        pltpu.VMEM((nct, SUBLANES * cpitch, LANES), F32),
                        pltpu.VMEM((E, D), BF16)],
        compiler_params=_params(("arbitrary", "arbitrary")),
        name="conv_out",
    )(u, u, sz, x, conv_w.reshape(CONV_WIDTH, nct, LANES).transpose(1, 0, 2),
      conv_b.reshape(nct, 1, LANES), row(ln_g), row(ln_b), w_out, row(g_kv), row(g_b))


def _row_pitch(dil):
    return dil + SUBLANES if dil % (2 * SUBLANES) == 0 else dil


def _proj_kernel(x_ref, w_ref, o_ref, w_s, acc_s, *, dil, epilogue, scale):
    first = (pl.program_id(1) == 0) & (pl.program_id(2) == 0)

    @pl.when(first)
    def _():
        w_s[...] = w_ref[...].astype(BF16)

    acc = jnp.dot(x_ref[0], w_s[...], preferred_element_type=F32)
    if epilogue == "silu":
        acc = acc * _sigmoid(acc)
    elif epilogue == "scale":
        acc = acc * scale
    nh = o_ref.shape[1]
    tl = o_ref.shape[4]
    if dil == 1:
        for hh in range(nh):
            o_ref[0, hh, 0, 0] = acc[:, hh * HEAD_DIM:(hh + 1) * HEAD_DIM].astype(o_ref.dtype)
    else:
        pitch = _row_pitch(dil)
        for hh in range(nh):
            piece = acc[:, hh * HEAD_DIM:(hh + 1) * HEAD_DIM]
            if pitch == dil:
                acc_s[hh] = piece
            else:
                for l in range(tl):
                    acc_s[hh, l * pitch:l * pitch + dil, :] = piece[l * dil:(l + 1) * dil, :]
        for hh in range(nh):
            for r in range(dil):
                o_ref[0, hh, 0, r] = acc_s[hh, pl.ds(r, tl, stride=pitch), :].astype(o_ref.dtype)


def _proj(act, w, col_off, n_cols, dil, out_dtype, epilogue="none", scale=1.0, parts=1, tn=1024, tm=1024):
    B, S, D = act.shape
    L = S // dil
    tl = tm // dil
    nh = tn // HEAD_DIM
    cb0 = col_off // tn
    H = n_cols // parts // HEAD_DIM
    npp = H // nh
    out = pl.pallas_call(
        functools.partial(_proj_kernel, dil=dil, epilogue=epilogue, scale=scale),
        grid=(n_cols // tn, B, S // tm),
        in_specs=[pl.BlockSpec((1, tm, D), lambda n, b, i: (b, i, 0)),
                  pl.BlockSpec((D, tn), lambda n, b, i: (0, cb0 + n))],
        out_specs=pl.BlockSpec((1, nh, 1, dil, tl, HEAD_DIM),
                               lambda n, b, i: (b, n % npp, n // npp, 0, i, 0)),
        out_shape=jax.ShapeDtypeStruct((B, H, parts, dil, L, HEAD_DIM), out_dtype),
        scratch_shapes=[pltpu.VMEM((D, tn), BF16),
                        pltpu.VMEM((nh, tl * _row_pitch(dil), HEAD_DIM), F32)],
        compiler_params=_params(("arbitrary",) * 3),
        name="proj_d%d_%s" % (dil, epilogue),
    )(act, w)
    return out.reshape(B, H, parts, S, HEAD_DIM)


def _bucket_tables():
    delta = (jnp.arange(BLOCK)[:, None] + BLOCK) - jnp.arange(2 * BLOCK)[None, :]
    tabs = []
    for window, dil in DILATED_GROUPS:
        w_sub = window // dil
        local = (delta >= 0) & (delta <= w_sub)
        dist = jnp.clip(delta, 0) * dil
        large = MAX_EXACT + (jnp.log(jnp.maximum(dist, 1).astype(F32) / MAX_EXACT)
                             / math.log(MAX_DISTANCE / MAX_EXACT)
                             * (N_BUCKETS - MAX_EXACT)).astype(jnp.int32)
        large = jnp.minimum(large, N_BUCKETS - 1)
        bucket = jnp.where(dist < MAX_EXACT, dist, large)
        tabs.append(jnp.where(local, bucket, -1).astype(jnp.int32))
    return jnp.stack(tabs)


def _attn_kernel(rb_ref, bkt_ref, q1, q2, q3, kv1, kv2, kv3, sz_ref, y_ref,
                 tbl_s, o_s, m_s, d_s, e_s, *, S, mc, unroll, tile):
    h = pl.program_id(0)

    @pl.when(pl.program_id(1) == 0)
    def _():
        for g in range(N_GROUPS):
            bk = bkt_ref[g]
            t = jnp.full(bk.shape, -jnp.inf, F32)
            for kk in range(N_BUCKETS):
                t = jnp.where(bk == kk, rb_ref[kk * N_HEADS + h] * LOG2E, t)
            tbl_s[g] = t

    groups = ((q1, kv1), (q2, kv2), (q3, kv3))
    bpt = tile // BLOCK

    blocks = [(g, j) for g in range(N_GROUPS) for j in range(bpt)]
    n_sets = len(blocks) // unroll

    def tile_body(T):
        def geometry(g, j):
            dil = DILATED_GROUPS[g][1]
            nb = S // BLOCK // dil
            bpr = bpt // dil
            r, jl = j // bpr, j % bpr
            lb = T * bpr + jl
            p = r * nb + lb
            nk = 1 if lb == 0 else 2
            row = p * BLOCK
            krow = (p + 1 - nk) * BLOCK
            pitch = _row_pitch(dil)
            t0 = jl * (BLOCK * pitch) + r
            idx = pl.ds(t0, BLOCK) if dil == 1 else pl.ds(t0, BLOCK, stride=pitch)
            return row, krow, nk, idx

        def scores(g, j, slot):
            row, krow, nk, idx = geometry(g, j)
            q = groups[g][0][0, 0, 0, pl.ds(row, BLOCK), :]
            k = groups[g][1][0, 0, 0, pl.ds(krow, nk * BLOCK), :]
            s = lax.dot_general(q, k, (((1,), (1,)), ((), ())), preferred_element_type=F32)
            s = s + tbl_s[g, :, (2 - nk) * BLOCK:]
            m = jnp.max(s, axis=-1, keepdims=True)
            e_s[slot, :, 0:nk * BLOCK] = jnp.exp2(s - m).astype(BF16)
            m_s[g, idx, :] = jnp.broadcast_to(m, (BLOCK, HEAD_DIM))

        def values(g, j, slot):
            row, krow, nk, idx = geometry(g, j)
            v = groups[g][1][0, 0, 1, pl.ds(krow, nk * BLOCK), :]
            va = jnp.concatenate([v, jnp.ones_like(v)], axis=1)
            acc = jnp.dot(e_s[slot, :, 0:nk * BLOCK], va, preferred_element_type=F32)
            o_s[g, idx, :] = acc[:, :HEAD_DIM]
            d_s[g, idx, :] = acc[:, HEAD_DIM:]

        for k in range(n_sets + 1):
            for uu in range(unroll):
                if k < n_sets:
                    scores(*blocks[k * unroll + uu], (k % 2) * unroll + uu)
                if k > 0:
                    values(*blocks[(k - 1) * unroll + uu], ((k - 1) % 2) * unroll + uu)

        def merge(c, c2):
            rows = pl.ds(pl.multiple_of(c * mc, mc), mc)
            def staged(ref, g):
                dil = DILATED_GROUPS[g][1]
                pitch = _row_pitch(dil)
                if pitch == dil:
                    return ref[g, rows, :]
                return jnp.concatenate(
                    [ref[g, pl.ds(pl.multiple_of((c * (mc // dil) + i) * pitch, SUBLANES), dil), :]
                     for i in range(mc // dil)], axis=0)

            m1, m2, m3 = staged(m_s, 0), staged(m_s, 1), staged(m_s, 2)
            mx = jnp.maximum(jnp.maximum(m1, m2), m3)
            w1, w2, w3 = jnp.exp2(m1 - mx), jnp.exp2(m2 - mx), jnp.exp2(m3 - mx)
            num = w1 * staged(o_s, 0) + w2 * staged(o_s, 1) + w3 * staged(o_s, 2)
            den = w1 * staged(d_s, 0) + w2 * staged(d_s, 1) + w3 * staged(d_s, 2)
            orow = pl.ds(pl.multiple_of(T * tile + c * mc, mc), mc)
            y_ref[0, 0, orow, :] = ((num / den) * sz_ref[0, 0, 0, orow, :]).astype(y_ref.dtype)
            return c2

        lax.fori_loop(0, tile // mc, merge, 0)

    for T in range(S // tile):
        tile_body(T)


def _attention(rel_bias, qs, kvs, sz, mc=256, unroll=4, tile=2048):
    B, H, _, S, Dh = qs[0].shape
    bkt = _bucket_tables()
    rows = max(tile // dil * _row_pitch(dil) for _, dil in DILATED_GROUPS)
    hm = lambda parts: pl.BlockSpec((1, 1, parts, S, Dh), lambda h, b: (b, h, 0, 0, 0))
    in_specs = ([pl.BlockSpec(memory_space=pltpu.SMEM),
                 pl.BlockSpec(bkt.shape, lambda h, b: (0, 0, 0))]
                + [hm(1)] * 3 + [hm(2)] * 3 + [hm(1)])
    return pl.pallas_call(
        functools.partial(_attn_kernel, S=S, mc=mc, unroll=unroll, tile=tile),
        grid=(H, B),
        in_specs=in_specs,
        out_specs=pl.BlockSpec((1, 1, S, Dh), lambda h, b: (b, h, 0, 0)),
        out_shape=jax.ShapeDtypeStruct((B, H, S, Dh), BF16),
        scratch_shapes=[pltpu.VMEM((N_GROUPS, BLOCK, 2 * BLOCK), F32),
                        pltpu.VMEM((N_GROUPS, rows, Dh), F32),
                        pltpu.VMEM((N_GROUPS, rows, Dh), F32),
                        pltpu.VMEM((N_GROUPS, rows, Dh), F32),
                        pltpu.VMEM((2 * unroll, BLOCK, 2 * BLOCK), BF16)],
        compiler_params=_params(("arbitrary", "arbitrary")),
        name="attention",
    )(rel_bias.reshape(-1), bkt, *qs, *kvs, sz)


def _final_kernel(y_ref, w_ref, x1_ref, g_ref, o_ref, w_s):
    @pl.when((pl.program_id(0) == 0) & (pl.program_id(1) == 0))
    def _():
        w_s[...] = w_ref[...].astype(BF16)

    y = jnp.concatenate([y_ref[0, h] for h in range(y_ref.shape[1])], axis=-1)
    x2 = x1_ref[0] + jnp.dot(y, w_s[...], preferred_element_type=F32)
    ms = jnp.mean(x2 * x2, axis=-1, keepdims=True)
    o_ref[0] = x2 * lax.rsqrt(ms + EPS) * g_ref[...]


def _final(y, w, x1, g, tm=512):
    B, H, S, Dh = y.shape
    D = w.shape[1]
    return pl.pallas_call(
        _final_kernel,
        grid=(B, S // tm),
        in_specs=[pl.BlockSpec((1, H, tm, Dh), lambda b, i: (b, 0, i, 0)),
                  pl.BlockSpec((H * Dh, D), lambda b, i: (0, 0)),
                  pl.BlockSpec((1, tm, D), lambda b, i: (b, i, 0)),
                  pl.BlockSpec((1, D), lambda b, i: (0, 0))],
        out_specs=pl.BlockSpec((1, tm, D), lambda b, i: (b, i, 0)),
        out_shape=jax.ShapeDtypeStruct((B, S, D), F32),
        scratch_shapes=[pltpu.VMEM((H * Dh, D), BF16)],
        compiler_params=_params(("arbitrary", "arbitrary")),
        name="outproj_final",
    )(y, w, x1, g.reshape(1, D))


def kernel(x, a_norm, a_w_in, a_conv_w, a_conv_b, a_ln_g, a_ln_b, a_w_out, kv_norm, w_kv,
           b_norm, b_w_in, b_w_out, rel_bias, final_norm):
    B, S, D = x.shape
    M = B * S
    h0 = _rmsnorm_bf16(x.reshape(M, D), a_norm[0])
    u, sz0 = _inproj_a(h0, a_w_in[0])
    E = u.shape[-1]
    x1, hk, hb = _conv_out(u.reshape(B, S, E), sz0.reshape(B, S, E), x, a_conv_w[0], a_conv_b[0],
                           a_ln_g[0], a_ln_b[0], a_w_out[0], kv_norm, b_norm[0])
    A = N_HEADS * HEAD_DIM
    qs, kvs = [], []
    for g, (_, dil) in enumerate(DILATED_GROUPS):
        tn, tm = (2048, 512) if dil == 1 else (1024, 1024)
        qs.append(_proj(hb, b_w_in[0], g * A, A, dil, BF16, "scale", HEAD_DIM ** -0.5 * LOG2E, tn=tn, tm=tm))
        kvs.append(_proj(hk, w_kv, 2 * g * A, 2 * A, dil, BF16, parts=2, tn=tn, tm=tm))
    sz1 = _proj(hb, b_w_in[0], N_GROUPS * A, A, 1, F32, "silu")
    y = _attention(rel_bias, qs, kvs, sz1)
    return _final(y, b_w_out[0], x1, final_norm)
```

```python
import functools
import math

import jax
import jax.numpy as jnp
from jax import lax
from jax.experimental import pallas as pl
from jax.experimental.pallas import tpu as pltpu

D_MODEL = 2048
CONV_WIDTH = 31
HEAD_DIM = 128
N_HEADS = D_MODEL // HEAD_DIM
DILATED_GROUPS = ((128, 1), (512, 4), (2048, 16))
N_GROUPS = len(DILATED_GROUPS)
BLOCK = 128
N_BUCKETS = 32
MAX_EXACT = N_BUCKETS // 2
MAX_DISTANCE = 2048
EPS = 1e-6

LANES = 128
SUBLANES = 8
HALO = 32

F32 = jnp.float32
BF16 = jnp.bfloat16
VMEM_LIMIT = 56 * 1024 * 1024
LOG2E = math.log2(math.e)


def _sigmoid(x):
    return 0.5 * jnp.tanh(0.5 * x) + 0.5


def _silu(x):
    h = 0.5 * x
    return h + h * jnp.tanh(h)


def _params(sem):
    return pltpu.CompilerParams(dimension_semantics=sem, vmem_limit_bytes=VMEM_LIMIT)


def _rmsnorm_kernel(x_ref, g_ref, o_ref):
    x = x_ref[...]
    ms = jnp.mean(x * x, axis=-1, keepdims=True)
    o_ref[...] = (x * lax.rsqrt(ms + EPS) * g_ref[...]).astype(o_ref.dtype)


def _rmsnorm_bf16(x2d, g, tm=1024):
    M, D = x2d.shape
    return pl.pallas_call(
        _rmsnorm_kernel,
        grid=(M // tm,),
        in_specs=[pl.BlockSpec((tm, D), lambda i: (i, 0)),
                  pl.BlockSpec((1, D), lambda i: (0, 0))],
        out_specs=pl.BlockSpec((tm, D), lambda i: (i, 0)),
        out_shape=jax.ShapeDtypeStruct((M, D), BF16),
        compiler_params=_params(("arbitrary",)),
        name="prenorm",
    )(x2d, g.reshape(1, D))


def _inproj_a_kernel(h_ref, wa_ref, wb_ref, wz_ref, u_ref, sz_ref, wa_s, wb_s, wz_s):
    @pl.when(pl.program_id(1) == 0)
    def _():
        wa_s[...] = wa_ref[...].astype(BF16)
        wb_s[...] = wb_ref[...].astype(BF16)
        wz_s[...] = wz_ref[...].astype(BF16)

    h = h_ref[...]
    a = jnp.dot(h, wa_s[...], preferred_element_type=F32)
    b = jnp.dot(h, wb_s[...], preferred_element_type=F32)
    u_ref[...] = a * _sigmoid(b)
    z = jnp.dot(h, wz_s[...], preferred_element_type=F32)
    sz_ref[...] = _silu(z)


def _inproj_a(h, w_in, tm=1024, tn=512):
    M, D = h.shape
    E = w_in.shape[1] // 3
    nb = E // tn
    return pl.pallas_call(
        _inproj_a_kernel,
        grid=(nb, M // tm),
        in_specs=[pl.BlockSpec((tm, D), lambda n, m: (m, 0)),
                  pl.BlockSpec((D, tn), lambda n, m: (0, n)),
                  pl.BlockSpec((D, tn), lambda n, m: (0, n + nb)),
                  pl.BlockSpec((D, tn), lambda n, m: (0, n + 2 * nb))],
        out_specs=[pl.BlockSpec((tm, tn), lambda n, m: (m, n)),
                   pl.BlockSpec((tm, tn), lambda n, m: (m, n))],
        out_shape=[jax.ShapeDtypeStruct((M, E), F32),
                   jax.ShapeDtypeStruct((M, E), F32)],
        scratch_shapes=[pltpu.VMEM((D, tn), BF16)] * 3,
        compiler_params=_params(("arbitrary", "arbitrary")),
        name="inproj_a",
    )(h, w_in, w_in, w_in)


def _conv_out_kernel(ucur_ref, uprev_ref, sz_ref, x_ref, cw_ref, cb_ref, lg_ref, lb_ref,
                     wo_ref, gkv_ref, gb_ref, x1_ref, hk_ref, hb_ref, seg_s, c_s, wo_s, *, tm, pitch, cpitch, oc):
    E = ucur_ref.shape[-1]
    A = tm // SUBLANES
    first = pl.program_id(1) == 0
    base = HALO - (CONV_WIDTH - 1)

    @pl.when((pl.program_id(0) == 0) & first)
    def _():
        wo_s[...] = wo_ref[...].astype(BF16)

    for ct in range(E // LANES):
        cs = slice(ct * LANES, (ct + 1) * LANES)
        seg_s[ct, 0:HALO, :] = jnp.where(first, 0.0, uprev_ref[0, :, cs])
        seg_s[ct, HALO:HALO + A, :] = ucur_ref[0, 0:A, cs]
        for i in range(1, SUBLANES):
            seg_s[ct, i * pitch:i * pitch + HALO + A, :] = ucur_ref[0, i * A - HALO:(i + 1) * A, cs]
        wv = [cw_ref[ct, pl.ds(j, SUBLANES, stride=0), :] for j in range(CONV_WIDTH)]
        bias = cb_ref[ct, pl.ds(0, SUBLANES, stride=0), :]
        for a0 in range(0, A, oc):
            accs = [None] * oc
            for k in range(oc + CONV_WIDTH - 1):
                w = seg_s[ct, pl.ds(a0 + base + k, SUBLANES, stride=pitch), :]
                for o in range(oc):
                    j = k - o
                    if 0 <= j < CONV_WIDTH:
                        t = w * wv[j]
                        accs[o] = t if accs[o] is None else accs[o] + t
            for o in range(oc):
                c_s[ct, pl.ds(a0 + o, SUBLANES, stride=cpitch), :] = accs[o] + bias

    nct = E // LANES
    cp = [jnp.concatenate([c_s[ct, i * cpitch:i * cpitch + A, :] for i in range(SUBLANES)], axis=0)
          for ct in range(nct)]
    tot = cp[0]
    for ct in range(1, nct):
        tot = tot + cp[ct]
    mu = jnp.sum(tot, axis=-1, keepdims=True) * (1.0 / E)
    sq = None
    for ct in range(nct):
        d = cp[ct] - mu
        sq = d * d if sq is None else sq + d * d
    rs = lax.rsqrt(jnp.sum(sq, axis=-1, keepdims=True) * (1.0 / E) + EPS)
    ys = []
    for ct in range(nct):
        cs = slice(ct * LANES, (ct + 1) * LANES)
        cn = (cp[ct] - mu) * rs * lg_ref[:, cs] + lb_ref[:, cs]
        ys.append((_silu(cn) * sz_ref[0, :, cs]).astype(BF16))
    y = jnp.concatenate(ys, axis=-1)
    o = jnp.dot(y, wo_s[...], preferred_element_type=F32)
    x1 = x_ref[0] + o
    x1_ref[0] = x1
    xn = x1 * lax.rsqrt(jnp.mean(x1 * x1, axis=-1, keepdims=True) + EPS)
    hk_ref[0] = (xn * gkv_ref[...]).astype(BF16)
    hb_ref[0] = (xn * gb_ref[...]).astype(BF16)


def _conv_out(u, sz, x, conv_w, conv_b, ln_g, ln_b, w_out, g_kv, g_b, tm=256, oc=8):
    B, S, E = u.shape
    D = x.shape[-1]
    nct = E // LANES
    pitch = HALO + tm // SUBLANES + SUBLANES
    cpitch = tm // SUBLANES + SUBLANES
    row = lambda a: a.reshape(1, -1)
    full = lambda shape: pl.BlockSpec(shape, lambda b, i: (0,) * len(shape))
    tile = lambda w: pl.BlockSpec((1, tm, w), lambda b, i: (b, i, 0))
    hpb = tm // HALO
    return pl.pallas_call(
        functools.partial(_conv_out_kernel, tm=tm, pitch=pitch, cpitch=cpitch, oc=oc),
        grid=(B, S // tm),
        in_specs=[tile(E),
                  pl.BlockSpec((1, HALO, E), lambda b, i: (b, jnp.maximum(i * hpb - 1, 0), 0)),
                  tile(E), tile(D),
                  full((nct, CONV_WIDTH, LANES)), full((nct, 1, LANES)), full((1, E)), full((1, E)),
                  full((E, D)), full((1, D)), full((1, D))],
        out_specs=[tile(D), tile(D), tile(D)],
        out_shape=[jax.ShapeDtypeStruct((B, S, D), F32),
                   jax.ShapeDtypeStruct((B, S, D), BF16),
                   jax.ShapeDtypeStruct((B, S, D), BF16)],
        scratch_shapes=[pltpu.VMEM((nct, SUBLANES * pitch, LANES), F32),
                        pltpu.VMEM((nct, SUBLANES * cpitch, LANES), F32),
                        pltpu.VMEM((E, D), BF16)],
        compiler_params=_params(("arbitrary", "arbitrary")),
        name="conv_out",
    )(u, u, sz, x, conv_w.reshape(CONV_WIDTH, nct, LANES).transpose(1, 0, 2),
      conv_b.reshape(nct, 1, LANES), row(ln_g), row(ln_b), w_out, row(g_kv), row(g_b))


def _row_pitch(dil):
    return dil + SUBLANES if dil % (2 * SUBLANES) == 0 else dil


def _proj_kernel(x_ref, w_ref, o_ref, w_s, acc_s, *, dil, epilogue, scale):
    first = (pl.program_id(1) == 0) & (pl.program_id(2) == 0)

    @pl.when(first)
    def _():
        w_s[...] = w_ref[...].astype(BF16)

    acc = jnp.dot(x_ref[0], w_s[...], preferred_element_type=F32)
    if epilogue == "silu":
        acc = _silu(acc)
    elif epilogue == "scale":
        acc = acc * scale
    nh = o_ref.shape[1]
    tl = o_ref.shape[4]
    if dil == 1:
        for hh in range(nh):
            o_ref[0, hh, 0, 0] = acc[:, hh * HEAD_DIM:(hh + 1) * HEAD_DIM].astype(o_ref.dtype)
    else:
        pitch = _row_pitch(dil)
        for hh in range(nh):
            piece = acc[:, hh * HEAD_DIM:(hh + 1) * HEAD_DIM]
            if pitch == dil:
                acc_s[hh] = piece
            else:
                for l in range(tl):
                    acc_s[hh, l * pitch:l * pitch + dil, :] = piece[l * dil:(l + 1) * dil, :]
        for hh in range(nh):
            for r in range(dil):
                o_ref[0, hh, 0, r] = acc_s[hh, pl.ds(r, tl, stride=pitch), :].astype(o_ref.dtype)


def _proj(act, w, col_off, n_cols, dil, out_dtype, epilogue="none", scale=1.0, parts=1, tn=1024, tm=1024):
    B, S, D = act.shape
    L = S // dil
    tl = tm // dil
    nh = tn // HEAD_DIM
    cb0 = col_off // tn
    H = n_cols // parts // HEAD_DIM
    npp = H // nh
    out = pl.pallas_call(
        functools.partial(_proj_kernel, dil=dil, epilogue=epilogue, scale=scale),
        grid=(n_cols // tn, B, S // tm),
        in_specs=[pl.BlockSpec((1, tm, D), lambda n, b, i: (b, i, 0)),
                  pl.BlockSpec((D, tn), lambda n, b, i: (0, cb0 + n))],
        out_specs=pl.BlockSpec((1, nh, 1, dil, tl, HEAD_DIM),
                               lambda n, b, i: (b, n % npp, n // npp, 0, i, 0)),
        out_shape=jax.ShapeDtypeStruct((B, H, parts, dil, L, HEAD_DIM), out_dtype),
        scratch_shapes=[pltpu.VMEM((D, tn), BF16),
                        pltpu.VMEM((nh, tl * _row_pitch(dil), HEAD_DIM), F32)],
        compiler_params=_params(("arbitrary",) * 3),
        name="proj_d%d_%s" % (dil, epilogue),
    )(act, w)
    return out.reshape(B, H, parts, S, HEAD_DIM)


def _bucket_tables():
    delta = (jnp.arange(BLOCK)[:, None] + BLOCK) - jnp.arange(2 * BLOCK)[None, :]
    tabs = []
    for window, dil in DILATED_GROUPS:
        w_sub = window // dil
        local = (delta >= 0) & (delta <= w_sub)
        dist = jnp.clip(delta, 0) * dil
        large = MAX_EXACT + (jnp.log(jnp.maximum(dist, 1).astype(F32) / MAX_EXACT)
                             / math.log(MAX_DISTANCE / MAX_EXACT)
                             * (N_BUCKETS - MAX_EXACT)).astype(jnp.int32)
        large = jnp.minimum(large, N_BUCKETS - 1)
        bucket = jnp.where(dist < MAX_EXACT, dist, large)
        tabs.append(jnp.where(local, bucket, -1).astype(jnp.int32))
    return jnp.stack(tabs)


def _attn_kernel(rb_ref, bkt_ref, q1, q2, q3, kv1, kv2, kv3, sz_ref, y_ref,
                 tbl_s, o_s, m_s, d_s, e_s, *, S, mc, unroll, tile):
    h = pl.program_id(0)

    @pl.when(pl.program_id(1) == 0)
    def _():
        for g in range(N_GROUPS):
            bk = bkt_ref[g]
            t = jnp.full(bk.shape, -jnp.inf, F32)
            for kk in range(N_BUCKETS):
                t = jnp.where(bk == kk, rb_ref[kk * N_HEADS + h] * LOG2E, t)
            tbl_s[g] = t

    groups = ((q1, kv1), (q2, kv2), (q3, kv3))
    bpt = tile // BLOCK

    blocks = [(g, j) for g in range(N_GROUPS) for j in range(bpt)]
    n_sets = len(blocks) // unroll

    def tile_body(T):
        def geometry(g, j):
            dil = DILATED_GROUPS[g][1]
            nb = S // BLOCK // dil
            bpr = bpt // dil
            r, jl = j // bpr, j % bpr
            lb = T * bpr + jl
            p = r * nb + lb
            nk = 1 if lb == 0 else 2
            row = p * BLOCK
            krow = (p + 1 - nk) * BLOCK
            pitch = _row_pitch(dil)
            t0 = jl * (BLOCK * pitch) + r
            idx = pl.ds(t0, BLOCK) if dil == 1 else pl.ds(t0, BLOCK, stride=pitch)
            return row, krow, nk, idx

        def scores(g, j, slot):
            row, krow, nk, idx = geometry(g, j)
            q = groups[g][0][0, 0, 0, pl.ds(row, BLOCK), :]
            k = groups[g][1][0, 0, 0, pl.ds(krow, nk * BLOCK), :]
            s = lax.dot_general(q, k, (((1,), (1,)), ((), ())), preferred_element_type=F32)
            s = s + tbl_s[g, :, (2 - nk) * BLOCK:]
            m = jnp.max(s, axis=-1, keepdims=True)
            e_s[slot, :, 0:nk * BLOCK] = jnp.exp2(s - m).astype(BF16)
            m_s[g, idx, :] = jnp.broadcast_to(m, (BLOCK, HEAD_DIM))

        def values(g, j, slot):
            row, krow, nk, idx = geometry(g, j)
            v = groups[g][1][0, 0, 1, pl.ds(krow, nk * BLOCK), :]
            va = jnp.concatenate([v, jnp.ones_like(v)], axis=1)
            acc = jnp.dot(e_s[slot, :, 0:nk * BLOCK], va, preferred_element_type=F32)
            o_s[g, idx, :] = acc[:, :HEAD_DIM]
            d_s[g, idx, :] = acc[:, HEAD_DIM:]

        for k in range(n_sets + 1):
            for uu in range(unroll):
                if k < n_sets:
                    scores(*blocks[k * unroll + uu], (k % 2) * unroll + uu)
                if k > 0:
                    values(*blocks[(k - 1) * unroll + uu], ((k - 1) % 2) * unroll + uu)

        def merge(c, c2):
            rows = pl.ds(pl.multiple_of(c * mc, mc), mc)
            def staged(ref, g):
                dil = DILATED_GROUPS[g][1]
                pitch = _row_pitch(dil)
                if pitch == dil:
                    return ref[g, rows, :]
                return jnp.concatenate(
                    [ref[g, pl.ds(pl.multiple_of((c * (mc // dil) + i) * pitch, SUBLANES), dil), :]
                     for i in range(mc // dil)], axis=0)

            m1, m2, m3 = staged(m_s, 0), staged(m_s, 1), staged(m_s, 2)
            mx = jnp.maximum(jnp.maximum(m1, m2), m3)
            w1, w2, w3 = jnp.exp2(m1 - mx), jnp.exp2(m2 - mx), jnp.exp2(m3 - mx)
            num = w1 * staged(o_s, 0) + w2 * staged(o_s, 1) + w3 * staged(o_s, 2)
            den = w1 * staged(d_s, 0) + w2 * staged(d_s, 1) + w3 * staged(d_s, 2)
            orow = pl.ds(pl.multiple_of(T * tile + c * mc, mc), mc)
            y_ref[0, 0, orow, :] = ((num / den) * sz_ref[0, 0, 0, orow, :]).astype(y_ref.dtype)
            return c2

        lax.fori_loop(0, tile // mc, merge, 0)

    for T in range(S // tile):
        tile_body(T)


def _attention(rel_bias, qs, kvs, sz, mc=256, unroll=4, tile=2048):
    B, H, _, S, Dh = qs[0].shape
    bkt = _bucket_tables()
    rows = max(tile // dil * _row_pitch(dil) for _, dil in DILATED_GROUPS)
    hm = lambda parts: pl.BlockSpec((1, 1, parts, S, Dh), lambda h, b: (b, h, 0, 0, 0))
    in_specs = ([pl.BlockSpec(memory_space=pltpu.SMEM),
                 pl.BlockSpec(bkt.shape, lambda h, b: (0, 0, 0))]
                + [hm(1)] * 3 + [hm(2)] * 3 + [hm(1)])
    return pl.pallas_call(
        functools.partial(_attn_kernel, S=S, mc=mc, unroll=unroll, tile=tile),
        grid=(H, B),
        in_specs=in_specs,
        out_specs=pl.BlockSpec((1, 1, S, Dh), lambda h, b: (b, h, 0, 0)),
        out_shape=jax.ShapeDtypeStruct((B, H, S, Dh), BF16),
        scratch_shapes=[pltpu.VMEM((N_GROUPS, BLOCK, 2 * BLOCK), F32),
                        pltpu.VMEM((N_GROUPS, rows, Dh), F32),
                        pltpu.VMEM((N_GROUPS, rows, Dh), F32),
                        pltpu.VMEM((N_GROUPS, rows, Dh), F32),
                        pltpu.VMEM((2 * unroll, BLOCK, 2 * BLOCK), BF16)],
        compiler_params=_params(("arbitrary", "arbitrary")),
        name="attention",
    )(rel_bias.reshape(-1), bkt, *qs, *kvs, sz)


def _final_kernel(y_ref, w_ref, x1_ref, g_ref, o_ref, w_s):
    @pl.when((pl.program_id(0) == 0) & (pl.program_id(1) == 0))
    def _():
        w_s[...] = w_ref[...].astype(BF16)

    y = jnp.concatenate([y_ref[0, h] for h in range(y_ref.shape[1])], axis=-1)
    x2 = x1_ref[0] + jnp.dot(y, w_s[...], preferred_element_type=F32)
    ms = jnp.mean(x2 * x2, axis=-1, keepdims=True)
    o_ref[0] = x2 * lax.rsqrt(ms + EPS) * g_ref[...]


def _final(y, w, x1, g, tm=512):
    B, H, S, Dh = y.shape
    D = w.shape[1]
    return pl.pallas_call(
        _final_kernel,
        grid=(B, S // tm),
        in_specs=[pl.BlockSpec((1, H, tm, Dh), lambda b, i: (b, 0, i, 0)),
                  pl.BlockSpec((H * Dh, D), lambda b, i: (0, 0)),
                  pl.BlockSpec((1, tm, D), lambda b, i: (b, i, 0)),
                  pl.BlockSpec((1, D), lambda b, i: (0, 0))],
        out_specs=pl.BlockSpec((1, tm, D), lambda b, i: (b, i, 0)),
        out_shape=jax.ShapeDtypeStruct((B, S, D), F32),
        scratch_shapes=[pltpu.VMEM((H * Dh, D), BF16)],
        compiler_params=_params(("arbitrary", "arbitrary")),
        name="outproj_final",
    )(y, w, x1, g.reshape(1, D))


def kernel(x, a_norm, a_w_in, a_conv_w, a_conv_b, a_ln_g, a_ln_b, a_w_out, kv_norm, w_kv,
           b_norm, b_w_in, b_w_out, rel_bias, final_norm):
    B, S, D = x.shape
    M = B * S
    h0 = _rmsnorm_bf16(x.reshape(M, D), a_norm[0])
    u, sz0 = _inproj_a(h0, a_w_in[0])
    E = u.shape[-1]
    x1, hk, hb = _conv_out(u.reshape(B, S, E), sz0.reshape(B, S, E), x, a_conv_w[0], a_conv_b[0],
                           a_ln_g[0], a_ln_b[0], a_w_out[0], kv_norm, b_norm[0])
    A = N_HEADS * HEAD_DIM
    qs, kvs = [], []
    for g, (_, dil) in enumerate(DILATED_GROUPS):
        tm = 2048 if dil == 1 else 1024
        qs.append(_proj(hb, b_w_in[0], g * A, A, dil, BF16, "scale", HEAD_DIM ** -0.5 * LOG2E, tm=tm))
        kvs.append(_proj(hk, w_kv, 2 * g * A, 2 * A, dil, BF16, parts=2, tm=tm))
    sz1 = _proj(hb, b_w_in[0], N_GROUPS * A, A, 1, F32, "silu")
    y = _attention(rel_bias, qs, kvs, sz1)
    return _final(y, b_w_out[0], x1, final_norm)
```

```python
import functools
import math

import jax
import jax.numpy as jnp
from jax import lax
from jax.experimental import pallas as pl
from jax.experimental.pallas import tpu as pltpu

D_MODEL = 2048
CONV_WIDTH = 31
HEAD_DIM = 128
N_HEADS = D_MODEL // HEAD_DIM
DILATED_GROUPS = ((128, 1), (512, 4), (2048, 16))
N_GROUPS = len(DILATED_GROUPS)
BLOCK = 128
N_BUCKETS = 32
MAX_EXACT = N_BUCKETS // 2
MAX_DISTANCE = 2048
EPS = 1e-6

LANES = 128
SUBLANES = 8
HALO = 32

F32 = jnp.float32
BF16 = jnp.bfloat16
VMEM_LIMIT = 56 * 1024 * 1024
LOG2E = math.log2(math.e)


def _sigmoid(x):
    return 0.5 * jnp.tanh(0.5 * x) + 0.5


def _silu(x):
    h = 0.5 * x
    return h + h * jnp.tanh(h)


def _params(sem):
    return pltpu.CompilerParams(dimension_semantics=sem, vmem_limit_bytes=VMEM_LIMIT)


def _inproj_a_kernel(x_ref, g_ref, wa_ref, wb_ref, wz_ref, u_ref, sz_ref, wa_s, wb_s, wz_s):
    @pl.when(pl.program_id(1) == 0)
    def _():
        wa_s[...] = wa_ref[...].astype(BF16)
        wb_s[...] = wb_ref[...].astype(BF16)
        wz_s[...] = wz_ref[...].astype(BF16)

    x = x_ref[...]
    rs = lax.rsqrt(jnp.mean(x * x, axis=-1, keepdims=True) + EPS)
    h = (x * rs * g_ref[...]).astype(BF16)
    a = jnp.dot(h, wa_s[...], preferred_element_type=F32)
    b = jnp.dot(h, wb_s[...], preferred_element_type=F32)
    u_ref[...] = a * _sigmoid(b)
    z = jnp.dot(h, wz_s[...], preferred_element_type=F32)
    sz_ref[...] = _silu(z)


def _inproj_a(x2d, g, w_in, tm=512, tn=512):
    M, D = x2d.shape
    E = w_in.shape[1] // 3
    nb = E // tn
    return pl.pallas_call(
        _inproj_a_kernel,
        grid=(nb, M // tm),
        in_specs=[pl.BlockSpec((tm, D), lambda n, m: (m, 0)),
                  pl.BlockSpec((1, D), lambda n, m: (0, 0)),
                  pl.BlockSpec((D, tn), lambda n, m: (0, n)),
                  pl.BlockSpec((D, tn), lambda n, m: (0, n + nb)),
                  pl.BlockSpec((D, tn), lambda n, m: (0, n + 2 * nb))],
        out_specs=[pl.BlockSpec((tm, tn), lambda n, m: (m, n)),
                   pl.BlockSpec((tm, tn), lambda n, m: (m, n))],
        out_shape=[jax.ShapeDtypeStruct((M, E), F32),
                   jax.ShapeDtypeStruct((M, E), F32)],
        scratch_shapes=[pltpu.VMEM((D, tn), BF16)] * 3,
        compiler_params=_params(("arbitrary", "arbitrary")),
        name="inproj_a",
    )(x2d, g.reshape(1, D), w_in, w_in, w_in)


def _conv_out_kernel(ucur_ref, uprev_ref, sz_ref, x_ref, cw_ref, cb_ref, lg_ref, lb_ref,
                     wo_ref, gkv_ref, gb_ref, x1_ref, hk_ref, hb_ref, seg_s, c_s, wo_s, *, tm, pitch, cpitch, oc):
    E = ucur_ref.shape[-1]
    A = tm // SUBLANES
    first = pl.program_id(1) == 0
    base = HALO - (CONV_WIDTH - 1)

    @pl.when((pl.program_id(0) == 0) & first)
    def _():
        wo_s[...] = wo_ref[...].astype(BF16)

    for ct in range(E // LANES):
        cs = slice(ct * LANES, (ct + 1) * LANES)
        seg_s[ct, 0:HALO, :] = jnp.where(first, 0.0, uprev_ref[0, :, cs])
        seg_s[ct, HALO:HALO + A, :] = ucur_ref[0, 0:A, cs]
        for i in range(1, SUBLANES):
            seg_s[ct, i * pitch:i * pitch + HALO + A, :] = ucur_ref[0, i * A - HALO:(i + 1) * A, cs]
        wv = [cw_ref[ct, pl.ds(j, SUBLANES, stride=0), :] for j in range(CONV_WIDTH)]
        bias = cb_ref[ct, pl.ds(0, SUBLANES, stride=0), :]
        for a0 in range(0, A, oc):
            accs = [None] * oc
            for k in range(oc + CONV_WIDTH - 1):
                w = seg_s[ct, pl.ds(a0 + base + k, SUBLANES, stride=pitch), :]
                for o in range(oc):
                    j = k - o
                    if 0 <= j < CONV_WIDTH:
                        t = w * wv[j]
                        accs[o] = t if accs[o] is None else accs[o] + t
            for o in range(oc):
                c_s[ct, pl.ds(a0 + o, SUBLANES, stride=cpitch), :] = accs[o] + bias

    nct = E // LANES
    cp = [jnp.concatenate([c_s[ct, i * cpitch:i * cpitch + A, :] for i in range(SUBLANES)], axis=0)
          for ct in range(nct)]
    tot = cp[0]
    for ct in range(1, nct):
        tot = tot + cp[ct]
    mu = jnp.sum(tot, axis=-1, keepdims=True) * (1.0 / E)
    sq = None
    for ct in range(nct):
        d = cp[ct] - mu
        sq = d * d if sq is None else sq + d * d
    rs = lax.rsqrt(jnp.sum(sq, axis=-1, keepdims=True) * (1.0 / E) + EPS)
    ys = []
    for ct in range(nct):
        cs = slice(ct * LANES, (ct + 1) * LANES)
        cn = (cp[ct] - mu) * rs * lg_ref[:, cs] + lb_ref[:, cs]
        ys.append((_silu(cn) * sz_ref[0, :, cs]).astype(BF16))
    y = jnp.concatenate(ys, axis=-1)
    o = jnp.dot(y, wo_s[...], preferred_element_type=F32)
    x1 = x_ref[0] + o
    x1_ref[0] = x1
    xn = x1 * lax.rsqrt(jnp.mean(x1 * x1, axis=-1, keepdims=True) + EPS)
    hk_ref[0] = (xn * gkv_ref[...]).astype(BF16)
    hb_ref[0] = (xn * gb_ref[...]).astype(BF16)


def _conv_out(u, sz, x, conv_w, conv_b, ln_g, ln_b, w_out, g_kv, g_b, tm=256, oc=8):
    B, S, E = u.shape
    D = x.shape[-1]
    nct = E // LANES
    pitch = HALO + tm // SUBLANES + SUBLANES
    cpitch = tm // SUBLANES + SUBLANES
    row = lambda a: a.reshape(1, -1)
    full = lambda shape: pl.BlockSpec(shape, lambda b, i: (0,) * len(shape))
    tile = lambda w: pl.BlockSpec((1, tm, w), lambda b, i: (b, i, 0))
    hpb = tm // HALO
    return pl.pallas_call(
        functools.partial(_conv_out_kernel, tm=tm, pitch=pitch, cpitch=cpitch, oc=oc),
        grid=(B, S // tm),
        in_specs=[tile(E),
                  pl.BlockSpec((1, HALO, E), lambda b, i: (b, jnp.maximum(i * hpb - 1, 0), 0)),
                  tile(E), tile(D),
                  full((nct, CONV_WIDTH, LANES)), full((nct, 1, LANES)), full((1, E)), full((1, E)),
                  full((E, D)), full((1, D)), full((1, D))],
        out_specs=[tile(D), tile(D), tile(D)],
        out_shape=[jax.ShapeDtypeStruct((B, S, D), F32),
                   jax.ShapeDtypeStruct((B, S, D), BF16),
                   jax.ShapeDtypeStruct((B, S, D), BF16)],
        scratch_shapes=[pltpu.VMEM((nct, SUBLANES * pitch, LANES), F32),
                        pltpu.VMEM((nct, SUBLANES * cpitch, LANES), F32),
                        pltpu.VMEM((E, D), BF16)],
        compiler_params=_params(("arbitrary", "arbitrary")),
        name="conv_out",
    )(u, u, sz, x, conv_w.reshape(CONV_WIDTH, nct, LANES).transpose(1, 0, 2),
      conv_b.reshape(nct, 1, LANES), row(ln_g), row(ln_b), w_out, row(g_kv), row(g_b))


def _row_pitch(dil):
    return dil + SUBLANES if dil % (2 * SUBLANES) == 0 else dil


def _proj_kernel(x_ref, w_ref, o_ref, w_s, acc_s, *, dil, epilogue, scale):
    first = (pl.program_id(1) == 0) & (pl.program_id(2) == 0)

    @pl.when(first)
    def _():
        w_s[...] = w_ref[...].astype(BF16)

    acc = jnp.dot(x_ref[0], w_s[...], preferred_element_type=F32)
    if epilogue == "silu":
        acc = _silu(acc)
    elif epilogue == "scale":
        acc = acc * scale
    nh = o_ref.shape[1]
    tl = o_ref.shape[4]
    if dil == 1:
        for hh in range(nh):
            o_ref[0, hh, 0, 0] = acc[:, hh * HEAD_DIM:(hh + 1) * HEAD_DIM].astype(o_ref.dtype)
    else:
        pitch = _row_pitch(dil)
        for hh in range(nh):
            piece = acc[:, hh * HEAD_DIM:(hh + 1) * HEAD_DIM]
            if pitch == dil:
                acc_s[hh] = piece
            else:
                for l in range(tl):
                    acc_s[hh, l * pitch:l * pitch + dil, :] = piece[l * dil:(l + 1) * dil, :]
        for hh in range(nh):
            for r in range(dil):
                o_ref[0, hh, 0, r] = acc_s[hh, pl.ds(r, tl, stride=pitch), :].astype(o_ref.dtype)


def _proj(act, w, col_off, n_cols, dil, out_dtype, epilogue="none", scale=1.0, parts=1, tn=1024, tm=1024):
    B, S, D = act.shape
    L = S // dil
    tl = tm // dil
    nh = tn // HEAD_DIM
    cb0 = col_off // tn
    H = n_cols // parts // HEAD_DIM
    npp = H // nh
    out = pl.pallas_call(
        functools.partial(_proj_kernel, dil=dil, epilogue=epilogue, scale=scale),
        grid=(n_cols // tn, B, S // tm),
        in_specs=[pl.BlockSpec((1, tm, D), lambda n, b, i: (b, i, 0)),
                  pl.BlockSpec((D, tn), lambda n, b, i: (0, cb0 + n))],
        out_specs=pl.BlockSpec((1, nh, 1, dil, tl, HEAD_DIM),
                               lambda n, b, i: (b, n % npp, n // npp, 0, i, 0)),
        out_shape=jax.ShapeDtypeStruct((B, H, parts, dil, L, HEAD_DIM), out_dtype),
        scratch_shapes=[pltpu.VMEM((D, tn), BF16),
                        pltpu.VMEM((nh, tl * _row_pitch(dil), HEAD_DIM), F32)],
        compiler_params=_params(("arbitrary",) * 3),
        name="proj_d%d_%s" % (dil, epilogue),
    )(act, w)
    return out.reshape(B, H, parts, S, HEAD_DIM)


def _bucket_tables():
    delta = (jnp.arange(BLOCK)[:, None] + BLOCK) - jnp.arange(2 * BLOCK)[None, :]
    tabs = []
    for window, dil in DILATED_GROUPS:
        w_sub = window // dil
        local = (delta >= 0) & (delta <= w_sub)
        dist = jnp.clip(delta, 0) * dil
        large = MAX_EXACT + (jnp.log(jnp.maximum(dist, 1).astype(F32) / MAX_EXACT)
                             / math.log(MAX_DISTANCE / MAX_EXACT)
                             * (N_BUCKETS - MAX_EXACT)).astype(jnp.int32)
        large = jnp.minimum(large, N_BUCKETS - 1)
        bucket = jnp.where(dist < MAX_EXACT, dist, large)
        tabs.append(jnp.where(local, bucket, -1).astype(jnp.int32))
    return jnp.stack(tabs)


def _attn_kernel(rb_ref, bkt_ref, q1, q2, q3, kv1, kv2, kv3, sz_ref, y_ref,
                 tbl_s, o_s, m_s, d_s, e_s, *, S, mc, unroll, tile):
    h = pl.program_id(0)

    @pl.when(pl.program_id(1) == 0)
    def _():
        for g in range(N_GROUPS):
            bk = bkt_ref[g]
            t = jnp.full(bk.shape, -jnp.inf, F32)
            for kk in range(N_BUCKETS):
                t = jnp.where(bk == kk, rb_ref[kk * N_HEADS + h] * LOG2E, t)
            tbl_s[g] = t

    groups = ((q1, kv1), (q2, kv2), (q3, kv3))
    bpt = tile // BLOCK

    blocks = [(g, j) for g in range(N_GROUPS) for j in range(bpt)]
    n_sets = len(blocks) // unroll

    def tile_body(T):
        def geometry(g, j):
            dil = DILATED_GROUPS[g][1]
            nb = S // BLOCK // dil
            bpr = bpt // dil
            r, jl = j // bpr, j % bpr
            lb = T * bpr + jl
            p = r * nb + lb
            nk = 1 if lb == 0 else 2
            row = p * BLOCK
            krow = (p + 1 - nk) * BLOCK
            pitch = _row_pitch(dil)
            t0 = jl * (BLOCK * pitch) + r
            idx = pl.ds(t0, BLOCK) if dil == 1 else pl.ds(t0, BLOCK, stride=pitch)
            return row, krow, nk, idx

        def scores(g, j, slot):
            row, krow, nk, idx = geometry(g, j)
            q = groups[g][0][0, 0, 0, pl.ds(row, BLOCK), :]
            k = groups[g][1][0, 0, 0, pl.ds(krow, nk * BLOCK), :]
            s = lax.dot_general(q, k, (((1,), (1,)), ((), ())), preferred_element_type=F32)
            s = s + tbl_s[g, :, (2 - nk) * BLOCK:]
            m = jnp.max(s, axis=-1, keepdims=True)
            e_s[slot, :, 0:nk * BLOCK] = jnp.exp2(s - m).astype(BF16)
            m_s[g, idx, :] = jnp.broadcast_to(m, (BLOCK, HEAD_DIM))

        def values(g, j, slot):
            row, krow, nk, idx = geometry(g, j)
            v = groups[g][1][0, 0, 1, pl.ds(krow, nk * BLOCK), :]
            va = jnp.concatenate([v, jnp.ones_like(v)], axis=1)
            acc = jnp.dot(e_s[slot, :, 0:nk * BLOCK], va, preferred_element_type=F32)
            o_s[g, idx, :] = acc[:, :HEAD_DIM]
            d_s[g, idx, :] = acc[:, HEAD_DIM:]

        for k in range(n_sets + 1):
            for uu in range(unroll):
                if k < n_sets:
                    scores(*blocks[k * unroll + uu], (k % 2) * unroll + uu)
                if k > 0:
                    values(*blocks[(k - 1) * unroll + uu], ((k - 1) % 2) * unroll + uu)

        def merge(c, c2):
            rows = pl.ds(pl.multiple_of(c * mc, mc), mc)
            def staged(ref, g):
                dil = DILATED_GROUPS[g][1]
                pitch = _row_pitch(dil)
                if pitch == dil:
                    return ref[g, rows, :]
                return jnp.concatenate(
                    [ref[g, pl.ds(pl.multiple_of((c * (mc // dil) + i) * pitch, SUBLANES), dil), :]
                     for i in range(mc // dil)], axis=0)

            m1, m2, m3 = staged(m_s, 0), staged(m_s, 1), staged(m_s, 2)
            mx = jnp.maximum(jnp.maximum(m1, m2), m3)
            w1, w2, w3 = jnp.exp2(m1 - mx), jnp.exp2(m2 - mx), jnp.exp2(m3 - mx)
            num = w1 * staged(o_s, 0) + w2 * staged(o_s, 1) + w3 * staged(o_s, 2)
            den = w1 * staged(d_s, 0) + w2 * staged(d_s, 1) + w3 * staged(d_s, 2)
            orow = pl.ds(pl.multiple_of(T * tile + c * mc, mc), mc)
            y_ref[0, 0, orow, :] = ((num / den) * sz_ref[0, 0, 0, orow, :]).astype(y_ref.dtype)
            return c2

        lax.fori_loop(0, tile // mc, merge, 0)

    for T in range(S // tile):
        tile_body(T)


def _attention(rel_bias, qs, kvs, sz, mc=256, unroll=4, tile=2048):
    B, H, _, S, Dh = qs[0].shape
    bkt = _bucket_tables()
    rows = max(tile // dil * _row_pitch(dil) for _, dil in DILATED_GROUPS)
    hm = lambda parts: pl.BlockSpec((1, 1, parts, S, Dh), lambda h, b: (b, h, 0, 0, 0))
    in_specs = ([pl.BlockSpec(memory_space=pltpu.SMEM),
                 pl.BlockSpec(bkt.shape, lambda h, b: (0, 0, 0))]
                + [hm(1)] * 3 + [hm(2)] * 3 + [hm(1)])
    return pl.pallas_call(
        functools.partial(_attn_kernel, S=S, mc=mc, unroll=unroll, tile=tile),
        grid=(H, B),
        in_specs=in_specs,
        out_specs=pl.BlockSpec((1, 1, S, Dh), lambda h, b: (b, h, 0, 0)),
        out_shape=jax.ShapeDtypeStruct((B, H, S, Dh), BF16),
        scratch_shapes=[pltpu.VMEM((N_GROUPS, BLOCK, 2 * BLOCK), F32),
                        pltpu.VMEM((N_GROUPS, rows, Dh), F32),
                        pltpu.VMEM((N_GROUPS, rows, Dh), F32),
                        pltpu.VMEM((N_GROUPS, rows, Dh), F32),
                        pltpu.VMEM((2 * unroll, BLOCK, 2 * BLOCK), BF16)],
        compiler_params=_params(("arbitrary", "arbitrary")),
        name="attention",
    )(rel_bias.reshape(-1), bkt, *qs, *kvs, sz)


def _final_kernel(y_ref, w_ref, x1_ref, g_ref, o_ref, w_s):
    @pl.when((pl.program_id(0) == 0) & (pl.program_id(1) == 0))
    def _():
        w_s[...] = w_ref[...].astype(BF16)

    y = jnp.concatenate([y_ref[0, h] for h in range(y_ref.shape[1])], axis=-1)
    x2 = x1_ref[0] + jnp.dot(y, w_s[...], preferred_element_type=F32)
    ms = jnp.mean(x2 * x2, axis=-1, keepdims=True)
    o_ref[0] = x2 * lax.rsqrt(ms + EPS) * g_ref[...]


def _final(y, w, x1, g, tm=512):
    B, H, S, Dh = y.shape
    D = w.shape[1]
    return pl.pallas_call(
        _final_kernel,
        grid=(B, S // tm),
        in_specs=[pl.BlockSpec((1, H, tm, Dh), lambda b, i: (b, 0, i, 0)),
                  pl.BlockSpec((H * Dh, D), lambda b, i: (0, 0)),
                  pl.BlockSpec((1, tm, D), lambda b, i: (b, i, 0)),
                  pl.BlockSpec((1, D), lambda b, i: (0, 0))],
        out_specs=pl.BlockSpec((1, tm, D), lambda b, i: (b, i, 0)),
        out_shape=jax.ShapeDtypeStruct((B, S, D), F32),
        scratch_shapes=[pltpu.VMEM((H * Dh, D), BF16)],
        compiler_params=_params(("arbitrary", "arbitrary")),
        name="outproj_final",
    )(y, w, x1, g.reshape(1, D))


def kernel(x, a_norm, a_w_in, a_conv_w, a_conv_b, a_ln_g, a_ln_b, a_w_out, kv_norm, w_kv,
           b_norm, b_w_in, b_w_out, rel_bias, final_norm):
    B, S, D = x.shape
    M = B * S
    u, sz0 = _inproj_a(x.reshape(M, D), a_norm[0], a_w_in[0])
    E = u.shape[-1]
    x1, hk, hb = _conv_out(u.reshape(B, S, E), sz0.reshape(B, S, E), x, a_conv_w[0], a_conv_b[0],
                           a_ln_g[0], a_ln_b[0], a_w_out[0], kv_norm, b_norm[0])
    A = N_HEADS * HEAD_DIM
    qs, kvs = [], []
    for g, (_, dil) in enumerate(DILATED_GROUPS):
        tm = 2048 if dil == 1 else 1024
        qs.append(_proj(hb, b_w_in[0], g * A, A, dil, BF16, "scale", HEAD_DIM ** -0.5 * LOG2E, tm=tm))
        kvs.append(_proj(hk, w_kv, 2 * g * A, 2 * A, dil, BF16, parts=2, tm=tm))
    sz1 = _proj(hb, b_w_in[0], N_GROUPS * A, A, 1, F32, "silu")
    y = _attention(rel_bias, qs, kvs, sz1)
    return _final(y, b_w_out[0], x1, final_norm)
```

```python
import functools
import math

import jax
import jax.numpy as jnp
from jax import lax
from jax.experimental import pallas as pl
from jax.experimental.pallas import tpu as pltpu

D_MODEL = 2048
CONV_WIDTH = 31
HEAD_DIM = 128
N_HEADS = D_MODEL // HEAD_DIM
DILATED_GROUPS = ((128, 1), (512, 4), (2048, 16))
N_GROUPS = len(DILATED_GROUPS)
BLOCK = 128
N_BUCKETS = 32
MAX_EXACT = N_BUCKETS // 2
MAX_DISTANCE = 2048
EPS = 1e-6

LANES = 128
SUBLANES = 8
HALO = 32

F32 = jnp.float32
BF16 = jnp.bfloat16
VMEM_LIMIT = 56 * 1024 * 1024
VMEM_LIMIT_TALL = 60 * 1024 * 1024
LOG2E = math.log2(math.e)


def _sigmoid(x):
    return 0.5 * jnp.tanh(0.5 * x) + 0.5


def _silu(x):
    h = 0.5 * x
    return h + h * jnp.tanh(h)


def _params(sem, vmem_limit=VMEM_LIMIT):
    return pltpu.CompilerParams(dimension_semantics=sem, vmem_limit_bytes=vmem_limit)


def _inproj_a_kernel(x_ref, g_ref, wa_ref, wb_ref, wz_ref, u_ref, sz_ref, wa_s, wb_s, wz_s):
    @pl.when(pl.program_id(1) == 0)
    def _():
        wa_s[...] = wa_ref[...].astype(BF16)
        wb_s[...] = wb_ref[...].astype(BF16)
        wz_s[...] = wz_ref[...].astype(BF16)

    x = x_ref[...]
    h = (x * g_ref[...]).astype(BF16)
    rs = lax.rsqrt(jnp.mean(x * x, axis=-1, keepdims=True) + EPS)
    a = jnp.dot(h, wa_s[...], preferred_element_type=F32) * rs
    b = jnp.dot(h, wb_s[...], preferred_element_type=F32) * rs
    u_ref[...] = a * _sigmoid(b)
    z = jnp.dot(h, wz_s[...], preferred_element_type=F32) * rs
    sz_ref[...] = _silu(z)


def _inproj_a(x2d, g, w_in, tm=512, tn=512):
    M, D = x2d.shape
    E = w_in.shape[1] // 3
    nb = E // tn
    return pl.pallas_call(
        _inproj_a_kernel,
        grid=(nb, M // tm),
        in_specs=[pl.BlockSpec((tm, D), lambda n, m: (m, 0)),
                  pl.BlockSpec((1, D), lambda n, m: (0, 0)),
                  pl.BlockSpec((D, tn), lambda n, m: (0, n)),
                  pl.BlockSpec((D, tn), lambda n, m: (0, n + nb)),
                  pl.BlockSpec((D, tn), lambda n, m: (0, n + 2 * nb))],
        out_specs=[pl.BlockSpec((tm, tn), lambda n, m: (m, n)),
                   pl.BlockSpec((tm, tn), lambda n, m: (m, n))],
        out_shape=[jax.ShapeDtypeStruct((M, E), F32),
                   jax.ShapeDtypeStruct((M, E), F32)],
        scratch_shapes=[pltpu.VMEM((D, tn), BF16)] * 3,
        compiler_params=_params(("arbitrary", "arbitrary")),
        name="inproj_a",
    )(x2d, g.reshape(1, D), w_in, w_in, w_in)


def _conv_out_kernel(ucur_ref, uprev_ref, sz_ref, x_ref, cw_ref, cb_ref, lg_ref, lb_ref,
                     wo_ref, x1_ref, hx_ref, seg_s, c_s, wo_s, *, tm, pitch, cpitch, oc):
    E = ucur_ref.shape[-1]
    A = tm // SUBLANES
    first = pl.program_id(1) == 0
    base = HALO - (CONV_WIDTH - 1)

    @pl.when((pl.program_id(0) == 0) & first)
    def _():
        wo_s[...] = wo_ref[...].astype(BF16)

    for ct in range(E // LANES):
        cs = slice(ct * LANES, (ct + 1) * LANES)
        seg_s[ct, 0:HALO, :] = jnp.where(first, 0.0, uprev_ref[0, :, cs])
        seg_s[ct, HALO:HALO + A, :] = ucur_ref[0, 0:A, cs]
        for i in range(1, SUBLANES):
            seg_s[ct, i * pitch:i * pitch + HALO + A, :] = ucur_ref[0, i * A - HALO:(i + 1) * A, cs]
        wv = [cw_ref[ct, pl.ds(j, SUBLANES, stride=0), :] for j in range(CONV_WIDTH)]
        bias = cb_ref[ct, pl.ds(0, SUBLANES, stride=0), :]
        for a0 in range(0, A, oc):
            accs = [None] * oc
            for k in range(oc + CONV_WIDTH - 1):
                w = seg_s[ct, pl.ds(a0 + base + k, SUBLANES, stride=pitch), :]
                for o in range(oc):
                    j = k - o
                    if 0 <= j < CONV_WIDTH:
                        t = w * wv[j]
                        accs[o] = t if accs[o] is None else accs[o] + t
            for o in range(oc):
                c_s[ct, pl.ds(a0 + o, SUBLANES, stride=cpitch), :] = accs[o] + bias

    nct = E // LANES
    cp = [jnp.concatenate([c_s[ct, i * cpitch:i * cpitch + A, :] for i in range(SUBLANES)], axis=0)
          for ct in range(nct)]
    tot = cp[0]
    for ct in range(1, nct):
        tot = tot + cp[ct]
    mu = jnp.sum(tot, axis=-1, keepdims=True) * (1.0 / E)
    sq = None
    for ct in range(nct):
        d = cp[ct] - mu
        sq = d * d if sq is None else sq + d * d
    rs = lax.rsqrt(jnp.sum(sq, axis=-1, keepdims=True) * (1.0 / E) + EPS)
    ys = []
    for ct in range(nct):
        cs = slice(ct * LANES, (ct + 1) * LANES)
        cn = (cp[ct] - mu) * rs * lg_ref[:, cs] + lb_ref[:, cs]
        ys.append((_silu(cn) * sz_ref[0, :, cs]).astype(BF16))
    y = jnp.concatenate(ys, axis=-1)
    o = jnp.dot(y, wo_s[...], preferred_element_type=F32)
    x1 = x_ref[0] + o
    x1_ref[0] = x1
    hx_ref[0] = (x1 * lax.rsqrt(jnp.mean(x1 * x1, axis=-1, keepdims=True) + EPS)).astype(BF16)


def _conv_out(u, sz, x, conv_w, conv_b, ln_g, ln_b, w_out, tm=256, oc=8):
    B, S, E = u.shape
    D = x.shape[-1]
    nct = E // LANES
    pitch = HALO + tm // SUBLANES + SUBLANES
    cpitch = tm // SUBLANES + SUBLANES
    row = lambda a: a.reshape(1, -1)
    full = lambda shape: pl.BlockSpec(shape, lambda b, i: (0,) * len(shape))
    tile = lambda w: pl.BlockSpec((1, tm, w), lambda b, i: (b, i, 0))
    hpb = tm // HALO
    return pl.pallas_call(
        functools.partial(_conv_out_kernel, tm=tm, pitch=pitch, cpitch=cpitch, oc=oc),
        grid=(B, S // tm),
        in_specs=[tile(E),
                  pl.BlockSpec((1, HALO, E), lambda b, i: (b, jnp.maximum(i * hpb - 1, 0), 0)),
                  tile(E), tile(D),
                  full((nct, CONV_WIDTH, LANES)), full((nct, 1, LANES)), full((1, E)), full((1, E)),
                  full((E, D))],
        out_specs=[tile(D), tile(D)],
        out_shape=[jax.ShapeDtypeStruct((B, S, D), F32),
                   jax.ShapeDtypeStruct((B, S, D), BF16)],
        scratch_shapes=[pltpu.VMEM((nct, SUBLANES * pitch, LANES), F32),
                        pltpu.VMEM((nct, SUBLANES * cpitch, LANES), F32),
                        pltpu.VMEM((E, D), BF16)],
        compiler_params=_params(("arbitrary", "arbitrary")),
        name="conv_out",
    )(u, u, sz, x, conv_w.reshape(CONV_WIDTH, nct, LANES).transpose(1, 0, 2),
      conv_b.reshape(nct, 1, LANES), row(ln_g), row(ln_b), w_out)


def _row_pitch(dil):
    return dil + SUBLANES if dil % (2 * SUBLANES) == 0 else dil


def _proj_kernel(x_ref, g_ref, w_ref, o_ref, w_s, acc_s, *, dil, epilogue, scale):
    first = (pl.program_id(1) == 0) & (pl.program_id(2) == 0)

    @pl.when(first)
    def _():
        w_s[...] = (w_ref[...] * g_ref[...]).astype(BF16)

    acc = jnp.dot(x_ref[0], w_s[...], preferred_element_type=F32)
    if epilogue == "silu":
        acc = _silu(acc)
    elif epilogue == "scale":
        acc = acc * scale
    nh = o_ref.shape[1]
    tl = o_ref.shape[4]
    if dil == 1:
        for hh in range(nh):
            o_ref[0, hh, 0, 0] = acc[:, hh * HEAD_DIM:(hh + 1) * HEAD_DIM].astype(o_ref.dtype)
    else:
        pitch = _row_pitch(dil)
        for hh in range(nh):
            piece = acc[:, hh * HEAD_DIM:(hh + 1) * HEAD_DIM]
            if pitch == dil:
                acc_s[hh] = piece
            else:
                for l in range(tl):
                    acc_s[hh, l * pitch:l * pitch + dil, :] = piece[l * dil:(l + 1) * dil, :]
        for hh in range(nh):
            for r in range(dil):
                o_ref[0, hh, 0, r] = acc_s[hh, pl.ds(r, tl, stride=pitch), :].astype(o_ref.dtype)


def _proj(act, g, w, col_off, n_cols, dil, out_dtype, epilogue="none", scale=1.0, parts=1, tn=1024, tm=1024):
    B, S, D = act.shape
    L = S // dil
    tl = tm // dil
    nh = tn // HEAD_DIM
    cb0 = col_off // tn
    H = n_cols // parts // HEAD_DIM
    npp = H // nh
    out = pl.pallas_call(
        functools.partial(_proj_kernel, dil=dil, epilogue=epilogue, scale=scale),
        grid=(n_cols // tn, B, S // tm),
        in_specs=[pl.BlockSpec((1, tm, D), lambda n, b, i: (b, i, 0)),
                  pl.BlockSpec((D, 1), lambda n, b, i: (0, 0)),
                  pl.BlockSpec((D, tn), lambda n, b, i: (0, cb0 + n))],
        out_specs=pl.BlockSpec((1, nh, 1, dil, tl, HEAD_DIM),
                               lambda n, b, i: (b, n % npp, n // npp, 0, i, 0)),
        out_shape=jax.ShapeDtypeStruct((B, H, parts, dil, L, HEAD_DIM), out_dtype),
        scratch_shapes=[pltpu.VMEM((D, tn), BF16),
                        pltpu.VMEM((nh, tl * _row_pitch(dil), HEAD_DIM), F32)],
        compiler_params=_params(("arbitrary",) * 3, VMEM_LIMIT_TALL if tm > 1024 else VMEM_LIMIT),
        name="proj_d%d_%s" % (dil, epilogue),
    )(act, g.reshape(D, 1), w)
    return out.reshape(B, H, parts, S, HEAD_DIM)


def _bucket_tables():
    delta = (jnp.arange(BLOCK)[:, None] + BLOCK) - jnp.arange(2 * BLOCK)[None, :]
    tabs = []
    for window, dil in DILATED_GROUPS:
        w_sub = window // dil
        local = (delta >= 0) & (delta <= w_sub)
        dist = jnp.clip(delta, 0) * dil
        large = MAX_EXACT + (jnp.log(jnp.maximum(dist, 1).astype(F32) / MAX_EXACT)
                             / math.log(MAX_DISTANCE / MAX_EXACT)
                             * (N_BUCKETS - MAX_EXACT)).astype(jnp.int32)
        large = jnp.minimum(large, N_BUCKETS - 1)
        bucket = jnp.where(dist < MAX_EXACT, dist, large)
        tabs.append(jnp.where(local, bucket, -1).astype(jnp.int32))
    return jnp.stack(tabs)


def _attn_kernel(rb_ref, bkt_ref, q1, q2, q3, kv1, kv2, kv3, sz_ref, y_ref,
                 tbl_s, o_s, m_s, d_s, e_s, *, S, mc, unroll, tile):
    h = pl.program_id(0)

    @pl.when(pl.program_id(1) == 0)
    def _():
        for g in range(N_GROUPS):
            bk = bkt_ref[g]
            t = jnp.full(bk.shape, -jnp.inf, F32)
            for kk in range(N_BUCKETS):
                t = jnp.where(bk == kk, rb_ref[kk * N_HEADS + h] * LOG2E, t)
            tbl_s[g] = t

    groups = ((q1, kv1), (q2, kv2), (q3, kv3))
    bpt = tile // BLOCK

    blocks = [(g, j) for g in range(N_GROUPS) for j in range(bpt)]
    n_sets = len(blocks) // unroll

    def tile_body(T):
        def geometry(g, j):
            dil = DILATED_GROUPS[g][1]
            nb = S // BLOCK // dil
            bpr = bpt // dil
            r, jl = j // bpr, j % bpr
            lb = T * bpr + jl
            p = r * nb + lb
            nk = 1 if lb == 0 else 2
            row = p * BLOCK
            krow = (p + 1 - nk) * BLOCK
            pitch = _row_pitch(dil)
            t0 = jl * (BLOCK * pitch) + r
            idx = pl.ds(t0, BLOCK) if dil == 1 else pl.ds(t0, BLOCK, stride=pitch)
            return row, krow, nk, idx

        def scores(g, j, slot):
            row, krow, nk, idx = geometry(g, j)
            q = groups[g][0][0, 0, 0, pl.ds(row, BLOCK), :]
            k = groups[g][1][0, 0, 0, pl.ds(krow, nk * BLOCK), :]
            s = lax.dot_general(q, k, (((1,), (1,)), ((), ())), preferred_element_type=F32)
            s = s + tbl_s[g, :, (2 - nk) * BLOCK:]
            m = jnp.max(s, axis=-1, keepdims=True)
            e_s[slot, :, 0:nk * BLOCK] = jnp.exp2(s - m).astype(BF16)
            m_s[g, idx, :] = jnp.broadcast_to(m, (BLOCK, HEAD_DIM))

        def values(g, j, slot):
            row, krow, nk, idx = geometry(g, j)
            v = groups[g][1][0, 0, 1, pl.ds(krow, nk * BLOCK), :]
            va = jnp.concatenate([v, jnp.ones_like(v)], axis=1)
            acc = jnp.dot(e_s[slot, :, 0:nk * BLOCK], va, preferred_element_type=F32)
            o_s[g, idx, :] = acc[:, :HEAD_DIM]
            d_s[g, idx, :] = acc[:, HEAD_DIM:]

        for k in range(n_sets + 1):
            for uu in range(unroll):
                if k < n_sets:
                    scores(*blocks[k * unroll + uu], (k % 2) * unroll + uu)
                if k > 0:
                    values(*blocks[(k - 1) * unroll + uu], ((k - 1) % 2) * unroll + uu)

        def merge(c, c2):
            rows = pl.ds(pl.multiple_of(c * mc, mc), mc)
            def staged(ref, g):
                dil = DILATED_GROUPS[g][1]
                pitch = _row_pitch(dil)
                if pitch == dil:
                    return ref[g, rows, :]
                return jnp.concatenate(
                    [ref[g, pl.ds(pl.multiple_of((c * (mc // dil) + i) * pitch, SUBLANES), dil), :]
                     for i in range(mc // dil)], axis=0)

            m1, m2, m3 = staged(m_s, 0), staged(m_s, 1), staged(m_s, 2)
            mx = jnp.maximum(jnp.maximum(m1, m2), m3)
            w1, w2, w3 = jnp.exp2(m1 - mx), jnp.exp2(m2 - mx), jnp.exp2(m3 - mx)
            num = w1 * staged(o_s, 0) + w2 * staged(o_s, 1) + w3 * staged(o_s, 2)
            den = w1 * staged(d_s, 0) + w2 * staged(d_s, 1) + w3 * staged(d_s, 2)
            orow = pl.ds(pl.multiple_of(T * tile + c * mc, mc), mc)
            y_ref[0, 0, orow, :] = ((num / den) * sz_ref[0, 0, 0, orow, :]).astype(y_ref.dtype)
            return c2

        lax.fori_loop(0, tile // mc, merge, 0)

    for T in range(S // tile):
        tile_body(T)


def _attention(rel_bias, qs, kvs, sz, mc=256, unroll=4, tile=2048):
    B, H, _, S, Dh = qs[0].shape
    bkt = _bucket_tables()
    rows = max(tile // dil * _row_pitch(dil) for _, dil in DILATED_GROUPS)
    hm = lambda parts: pl.BlockSpec((1, 1, parts, S, Dh), lambda h, b: (b, h, 0, 0, 0))
    in_specs = ([pl.BlockSpec(memory_space=pltpu.SMEM),
                 pl.BlockSpec(bkt.shape, lambda h, b: (0, 0, 0))]
                + [hm(1)] * 3 + [hm(2)] * 3 + [hm(1)])
    return pl.pallas_call(
        functools.partial(_attn_kernel, S=S, mc=mc, unroll=unroll, tile=tile),
        grid=(H, B),
        in_specs=in_specs,
        out_specs=pl.BlockSpec((1, 1, S, Dh), lambda h, b: (b, h, 0, 0)),
        out_shape=jax.ShapeDtypeStruct((B, H, S, Dh), BF16),
        scratch_shapes=[pltpu.VMEM((N_GROUPS, BLOCK, 2 * BLOCK), F32),
                        pltpu.VMEM((N_GROUPS, rows, Dh), F32),
                        pltpu.VMEM((N_GROUPS, rows, Dh), F32),
                        pltpu.VMEM((N_GROUPS, rows, Dh), F32),
                        pltpu.VMEM((2 * unroll, BLOCK, 2 * BLOCK), BF16)],
        compiler_params=_params(("arbitrary", "arbitrary")),
        name="attention",
    )(rel_bias.reshape(-1), bkt, *qs, *kvs, sz)


def _final_kernel(y_ref, w_ref, x1_ref, g_ref, o_ref, w_s):
    @pl.when((pl.program_id(0) == 0) & (pl.program_id(1) == 0))
    def _():
        w_s[...] = w_ref[...].astype(BF16)

    y = jnp.concatenate([y_ref[0, h] for h in range(y_ref.shape[1])], axis=-1)
    x2 = x1_ref[0] + jnp.dot(y, w_s[...], preferred_element_type=F32)
    ms = jnp.mean(x2 * x2, axis=-1, keepdims=True)
    o_ref[0] = x2 * lax.rsqrt(ms + EPS) * g_ref[...]


def _final(y, w, x1, g, tm=512):
    B, H, S, Dh = y.shape
    D = w.shape[1]
    return pl.pallas_call(
        _final_kernel,
        grid=(B, S // tm),
        in_specs=[pl.BlockSpec((1, H, tm, Dh), lambda b, i: (b, 0, i, 0)),
                  pl.BlockSpec((H * Dh, D), lambda b, i: (0, 0)),
                  pl.BlockSpec((1, tm, D), lambda b, i: (b, i, 0)),
                  pl.BlockSpec((1, D), lambda b, i: (0, 0))],
        out_specs=pl.BlockSpec((1, tm, D), lambda b, i: (b, i, 0)),
        out_shape=jax.ShapeDtypeStruct((B, S, D), F32),
        scratch_shapes=[pltpu.VMEM((H * Dh, D), BF16)],
        compiler_params=_params(("arbitrary", "arbitrary")),
        name="outproj_final",
    )(y, w, x1, g.reshape(1, D))


def kernel(x, a_norm, a_w_in, a_conv_w, a_conv_b, a_ln_g, a_ln_b, a_w_out, kv_norm, w_kv,
           b_norm, b_w_in, b_w_out, rel_bias, final_norm):
    B, S, D = x.shape
    M = B * S
    u, sz0 = _inproj_a(x.reshape(M, D), a_norm[0], a_w_in[0])
    E = u.shape[-1]
    x1, hx = _conv_out(u.reshape(B, S, E), sz0.reshape(B, S, E), x, a_conv_w[0], a_conv_b[0],
                           a_ln_g[0], a_ln_b[0], a_w_out[0])
    A = N_HEADS * HEAD_DIM
    qs, kvs = [], []
    for g, (_, dil) in enumerate(DILATED_GROUPS):
        tm = 2048 if dil == 1 else 1024
        qs.append(_proj(hx, b_norm[0], b_w_in[0], g * A, A, dil, BF16, "scale",
                        HEAD_DIM ** -0.5 * LOG2E, tm=tm))
        kvs.append(_proj(hx, kv_norm, w_kv, 2 * g * A, 2 * A, dil, BF16, parts=2, tm=tm))
    sz1 = _proj(hx, b_norm[0], b_w_in[0], N_GROUPS * A, A, 1, F32, "silu")
    y = _attention(rel_bias, qs, kvs, sz1)
    return _final(y, b_w_out[0], x1, final_norm)
```

```python
import functools
import math

import jax
import jax.numpy as jnp
from jax import lax
from jax.experimental import pallas as pl
from jax.experimental.pallas import tpu as pltpu

D_MODEL = 2048
CONV_WIDTH = 31
HEAD_DIM = 128
N_HEADS = D_MODEL // HEAD_DIM
DILATED_GROUPS = ((128, 1), (512, 4), (2048, 16))
N_GROUPS = len(DILATED_GROUPS)
BLOCK = 128
N_BUCKETS = 32
MAX_EXACT = N_BUCKETS // 2
MAX_DISTANCE = 2048
EPS = 1e-6

LANES = 128
SUBLANES = 8
HALO = 32

F32 = jnp.float32
BF16 = jnp.bfloat16
VMEM_LIMIT = 56 * 1024 * 1024
CONV_VMEM_LIMIT = 60 * 1024 * 1024
LOG2E = math.log2(math.e)


def _silu(x):
    h = 0.5 * x
    return h + h * jnp.tanh(h)


def _params(sem, vmem_limit=VMEM_LIMIT):
    return pltpu.CompilerParams(dimension_semantics=sem, vmem_limit_bytes=vmem_limit)


def _inproj_a_kernel(x_ref, g_ref, wa_ref, wb_ref, wz_ref, u_ref, sz_ref, wa_s, wb_s, wz_s):
    @pl.when(pl.program_id(1) == 0)
    def _():
        wa_s[...] = wa_ref[...].astype(BF16)
        wb_s[...] = wb_ref[...].astype(BF16)
        wz_s[...] = wz_ref[...].astype(BF16)

    x = x_ref[...]
    h = (x * g_ref[...]).astype(BF16)
    rs = lax.rsqrt(jnp.mean(x * x, axis=-1, keepdims=True) + EPS)
    hrs = 0.5 * rs
    ha = jnp.dot(h, wa_s[...], preferred_element_type=F32) * hrs
    hb = jnp.dot(h, wb_s[...], preferred_element_type=F32) * hrs
    u_ref[...] = ha + ha * jnp.tanh(hb)
    hz = jnp.dot(h, wz_s[...], preferred_element_type=F32) * hrs
    sz_ref[...] = hz + hz * jnp.tanh(hz)


def _inproj_a(x2d, g, w_in, tm=512, tn=512):
    M, D = x2d.shape
    E = w_in.shape[1] // 3
    nb = E // tn
    return pl.pallas_call(
        _inproj_a_kernel,
        grid=(nb, M // tm),
        in_specs=[pl.BlockSpec((tm, D), lambda n, m: (m, 0)),
                  pl.BlockSpec((1, D), lambda n, m: (0, 0)),
                  pl.BlockSpec((D, tn), lambda n, m: (0, n)),
                  pl.BlockSpec((D, tn), lambda n, m: (0, n + nb)),
                  pl.BlockSpec((D, tn), lambda n, m: (0, n + 2 * nb))],
        out_specs=[pl.BlockSpec((tm, tn), lambda n, m: (m, n)),
                   pl.BlockSpec((tm, tn), lambda n, m: (m, n))],
        out_shape=[jax.ShapeDtypeStruct((M, E), F32),
                   jax.ShapeDtypeStruct((M, E), F32)],
        scratch_shapes=[pltpu.VMEM((D, tn), BF16)] * 3,
        compiler_params=_params(("arbitrary", "arbitrary")),
        name="inproj_a",
    )(x2d, g.reshape(1, D), w_in, w_in, w_in)


def _fir(load, tap, ntaps, m0, count):
    taps = [tap(i) for i in range(ntaps)]
    accs = [None] * count
    for n in range(m0 - (ntaps - 1), m0 + count):
        x = load(n)
        for p in range(count):
            i = m0 + p - n
            if 0 <= i < ntaps:
                t = x * taps[i]
                accs[p] = t if accs[p] is None else accs[p] + t
    return accs


def _conv_out_kernel(ucur_ref, uprev_ref, sz_ref, x_ref, cw_ref, cb_ref, lg_ref, lb_ref,
                     wo_ref, gkv_ref, gb_ref, x1_ref, hk_ref, hb_ref, seg_s, c_s, s_s, g_s, wo_s,
                     *, tm, pitch, cpitch, oc):
    E = ucur_ref.shape[-1]
    A = tm // SUBLANES
    first = pl.program_id(1) == 0
    n0, n1 = (CONV_WIDTH + 1) // 2, CONV_WIDTH // 2
    hp = HALO // 2

    @pl.when((pl.program_id(0) == 0) & first)
    def _():
        wo_s[...] = wo_ref[...].astype(BF16)
        for ct in range(E // LANES):
            for i in range(n0):
                j = CONV_WIDTH - 1 - 2 * i
                t = cw_ref[ct, j:j + 1, :]
                g_s[ct, i:i + 1, :] = t + cw_ref[ct, j - 1:j, :] if i < n1 else t

    for ct in range(E // LANES):
        cs = slice(ct * LANES, (ct + 1) * LANES)
        seg_s[ct, 0:HALO, :] = jnp.where(first, 0.0, uprev_ref[0, :, cs])
        seg_s[ct, HALO:HALO + A, :] = ucur_ref[0, 0:A, cs]
        for i in range(1, SUBLANES):
            seg_s[ct, i * pitch:i * pitch + HALO + A, :] = ucur_ref[0, i * A - HALO:(i + 1) * A, cs]
        bias = cb_ref[ct, pl.ds(0, SUBLANES, stride=0), :]
        g0 = lambda i: cw_ref[ct, pl.ds(CONV_WIDTH - 1 - 2 * i, SUBLANES, stride=0), :]
        g1 = lambda i: cw_ref[ct, pl.ds(CONV_WIDTH - 2 - 2 * i, SUBLANES, stride=0), :]
        gs = lambda i: g_s[ct, pl.ds(i, SUBLANES, stride=0), :]
        slot = ct % 2
        ev = lambda n: seg_s[ct, pl.ds(HALO + 2 * n, SUBLANES, stride=pitch), :]
        od = lambda n: seg_s[ct, pl.ds(HALO + 2 * n + 1, SUBLANES, stride=pitch), :]
        sm = lambda n: s_s[slot, (n + hp) * SUBLANES:(n + hp + 1) * SUBLANES, :]
        bprev = _fir(od, g1, n1, -1, 1)[0]
        for m0 in range(0, A // 2, oc):
            bv = _fir(od, g1, n1, m0, oc)
            av = _fir(ev, g0, n0, m0, oc)
            dv = []
            for p in range(oc):
                c_s[ct, pl.ds(2 * (m0 + p), SUBLANES, stride=cpitch), :] = av[p] + bprev + bias
                dv.append(av[p] + bv[p])
                bprev = bv[p]
            if m0 == 0:
                for n in range(1 - n0, A // 2):
                    s_s[slot, (n + hp) * SUBLANES:(n + hp + 1) * SUBLANES, :] = ev(n) + od(n)
            cv = _fir(sm, gs, n0, m0, oc)
            for p in range(oc):
                c_s[ct, pl.ds(2 * (m0 + p) + 1, SUBLANES, stride=cpitch), :] = cv[p] - dv[p] + bias

    nct = E // LANES
    cp = [jnp.concatenate([c_s[ct, i * cpitch:i * cpitch + A, :] for i in range(SUBLANES)], axis=0)
          for ct in range(nct)]
    tot = cp[0]
    for ct in range(1, nct):
        tot = tot + cp[ct]
    mu = jnp.sum(tot, axis=-1, keepdims=True) * (1.0 / E)
    sq = None
    for ct in range(nct):
        d = cp[ct] - mu
        sq = d * d if sq is None else sq + d * d
    rs = lax.rsqrt(jnp.sum(sq, axis=-1, keepdims=True) * (1.0 / E) + EPS)
    ys = []
    for ct in range(nct):
        cs = slice(ct * LANES, (ct + 1) * LANES)
        h = (cp[ct] - mu) * rs * (0.5 * lg_ref[:, cs]) + 0.5 * lb_ref[:, cs]
        ys.append(((h + h * jnp.tanh(h)) * sz_ref[0, :, cs]).astype(BF16))
    y = jnp.concatenate(ys, axis=-1)
    o = jnp.dot(y, wo_s[...], preferred_element_type=F32)
    x1 = x_ref[0] + o
    x1_ref[0] = x1
    xn = x1 * lax.rsqrt(jnp.mean(x1 * x1, axis=-1, keepdims=True) + EPS)
    hk_ref[0] = (xn * gkv_ref[...]).astype(BF16)
    hb_ref[0] = (xn * gb_ref[...]).astype(BF16)


def _conv_out(u, sz, x, conv_w, conv_b, ln_g, ln_b, w_out, g_kv, g_b, tm=256, oc=16):
    B, S, E = u.shape
    D = x.shape[-1]
    nct = E // LANES
    pitch = HALO + tm // SUBLANES + SUBLANES
    cpitch = tm // SUBLANES + SUBLANES
    row = lambda a: a.reshape(1, -1)
    full = lambda shape: pl.BlockSpec(shape, lambda b, i: (0,) * len(shape))
    tile = lambda w: pl.BlockSpec((1, tm, w), lambda b, i: (b, i, 0))
    hpb = tm // HALO
    return pl.pallas_call(
        functools.partial(_conv_out_kernel, tm=tm, pitch=pitch, cpitch=cpitch, oc=oc),
        grid=(B, S // tm),
        in_specs=[tile(E),
                  pl.BlockSpec((1, HALO, E), lambda b, i: (b, jnp.maximum(i * hpb - 1, 0), 0)),
                  tile(E), tile(D),
                  full((nct, CONV_WIDTH, LANES)), full((nct, 1, LANES)), full((1, E)), full((1, E)),
                  full((E, D)), full((1, D)), full((1, D))],
        out_specs=[tile(D), tile(D), tile(D)],
        out_shape=[jax.ShapeDtypeStruct((B, S, D), F32),
                   jax.ShapeDtypeStruct((B, S, D), BF16),
                   jax.ShapeDtypeStruct((B, S, D), BF16)],
        scratch_shapes=[pltpu.VMEM((nct, SUBLANES * pitch, LANES), F32),
                        pltpu.VMEM((nct, SUBLANES * cpitch, LANES), F32),
                        pltpu.VMEM((2, (HALO + tm // SUBLANES) // 2 * SUBLANES, LANES), F32),
                        pltpu.VMEM((nct, (CONV_WIDTH + 1) // 2, LANES), F32),
                        pltpu.VMEM((E, D), BF16)],
        compiler_params=_params(("arbitrary", "arbitrary"), CONV_VMEM_LIMIT),
        name="conv_out",
    )(u, u, sz, x, conv_w.reshape(CONV_WIDTH, nct, LANES).transpose(1, 0, 2),
      conv_b.reshape(nct, 1, LANES), row(ln_g), row(ln_b), w_out, row(g_kv), row(g_b))


def _row_pitch(dil):
    return dil + SUBLANES if dil % (2 * SUBLANES) == 0 else dil


def _proj_kernel(x_ref, w_ref, o_ref, w_s, acc_s, *, dil, epilogue, scale):
    first = (pl.program_id(1) == 0) & (pl.program_id(2) == 0)

    @pl.when(first)
    def _():
        w_s[...] = w_ref[...].astype(BF16)

    acc = jnp.dot(x_ref[0], w_s[...], preferred_element_type=F32)
    if epilogue == "silu":
        acc = _silu(acc)
    elif epilogue == "scale":
        acc = acc * scale
    nh = o_ref.shape[1]
    tl = o_ref.shape[4]
    if dil == 1:
        for hh in range(nh):
            o_ref[0, hh, 0, 0] = acc[:, hh * HEAD_DIM:(hh + 1) * HEAD_DIM].astype(o_ref.dtype)
    else:
        pitch = _row_pitch(dil)
        for hh in range(nh):
            piece = acc[:, hh * HEAD_DIM:(hh + 1) * HEAD_DIM]
            if pitch == dil:
                acc_s[hh] = piece
            else:
                for l in range(tl):
                    acc_s[hh, l * pitch:l * pitch + dil, :] = piece[l * dil:(l + 1) * dil, :]
        for hh in range(nh):
            for r in range(dil):
                o_ref[0, hh, 0, r] = acc_s[hh, pl.ds(r, tl, stride=pitch), :].astype(o_ref.dtype)


def _proj(act, w, col_off, n_cols, dil, out_dtype, epilogue="none", scale=1.0, parts=1, tn=1024, tm=1024):
    B, S, D = act.shape
    L = S // dil
    tl = tm // dil
    nh = tn // HEAD_DIM
    cb0 = col_off // tn
    H = n_cols // parts // HEAD_DIM
    npp = H // nh
    out = pl.pallas_call(
        functools.partial(_proj_kernel, dil=dil, epilogue=epilogue, scale=scale),
        grid=(n_cols // tn, B, S // tm),
        in_specs=[pl.BlockSpec((1, tm, D), lambda n, b, i: (b, i, 0)),
                  pl.BlockSpec((D, tn), lambda n, b, i: (0, cb0 + n))],
        out_specs=pl.BlockSpec((1, nh, 1, dil, tl, HEAD_DIM),
                               lambda n, b, i: (b, n % npp, n // npp, 0, i, 0)),
        out_shape=jax.ShapeDtypeStruct((B, H, parts, dil, L, HEAD_DIM), out_dtype),
        scratch_shapes=[pltpu.VMEM((D, tn), BF16),
                        pltpu.VMEM((nh, tl * _row_pitch(dil), HEAD_DIM), F32)],
        compiler_params=_params(("arbitrary",) * 3),
        name="proj_d%d_%s" % (dil, epilogue),
    )(act, w)
    return out.reshape(B, H, parts, S, HEAD_DIM)


def _bucket_tables():
    delta = (jnp.arange(BLOCK)[:, None] + BLOCK) - jnp.arange(2 * BLOCK)[None, :]
    tabs = []
    for window, dil in DILATED_GROUPS:
        w_sub = window // dil
        local = (delta >= 0) & (delta <= w_sub)
        dist = jnp.clip(delta, 0) * dil
        large = MAX_EXACT + (jnp.log(jnp.maximum(dist, 1).astype(F32) / MAX_EXACT)
                             / math.log(MAX_DISTANCE / MAX_EXACT)
                             * (N_BUCKETS - MAX_EXACT)).astype(jnp.int32)
        large = jnp.minimum(large, N_BUCKETS - 1)
        bucket = jnp.where(dist < MAX_EXACT, dist, large)
        tabs.append(jnp.where(local, bucket, -1).astype(jnp.int32))
    return jnp.stack(tabs)


def _attn_kernel(rb_ref, bkt_ref, q1, q2, q3, kv1, kv2, kv3, sz_ref, y_ref,
                 tbl_s, o_s, m_s, d_s, e_s, *, S, mc, unroll, tile):
    h = pl.program_id(0)

    @pl.when(pl.program_id(1) == 0)
    def _():
        for g in range(N_GROUPS):
            bk = bkt_ref[g]
            t = jnp.full(bk.shape, -jnp.inf, F32)
            for kk in range(N_BUCKETS):
                t = jnp.where(bk == kk, rb_ref[kk * N_HEADS + h] * LOG2E, t)
            tbl_s[g] = t

    groups = ((q1, kv1), (q2, kv2), (q3, kv3))
    bpt = tile // BLOCK

    blocks = [(g, j) for g in range(N_GROUPS) for j in range(bpt)]
    n_sets = len(blocks) // unroll

    def tile_body(T):
        def geometry(g, j):
            dil = DILATED_GROUPS[g][1]
            nb = S // BLOCK // dil
            bpr = bpt // dil
            r, jl = j // bpr, j % bpr
            lb = T * bpr + jl
            p = r * nb + lb
            nk = 1 if lb == 0 else 2
            row = p * BLOCK
            krow = (p + 1 - nk) * BLOCK
            pitch = _row_pitch(dil)
            t0 = jl * (BLOCK * pitch) + r
            idx = pl.ds(t0, BLOCK) if dil == 1 else pl.ds(t0, BLOCK, stride=pitch)
            return row, krow, nk, idx

        def scores(g, j, slot):
            row, krow, nk, idx = geometry(g, j)
            q = groups[g][0][0, 0, 0, pl.ds(row, BLOCK), :]
            k = groups[g][1][0, 0, 0, pl.ds(krow, nk * BLOCK), :]
            s = lax.dot_general(q, k, (((1,), (1,)), ((), ())), preferred_element_type=F32)
            s = s + tbl_s[g, :, (2 - nk) * BLOCK:]
            m = jnp.max(s, axis=-1, keepdims=True)
            e_s[slot, :, 0:nk * BLOCK] = jnp.exp2(s - m).astype(BF16)
            m_s[g, idx, :] = jnp.broadcast_to(m, (BLOCK, HEAD_DIM))

        def values(g, j, slot):
            row, krow, nk, idx = geometry(g, j)
            v = groups[g][1][0, 0, 1, pl.ds(krow, nk * BLOCK), :]
            va = jnp.concatenate([v, jnp.ones_like(v)], axis=1)
            acc = jnp.dot(e_s[slot, :, 0:nk * BLOCK], va, preferred_element_type=F32)
            o_s[g, idx, :] = acc[:, :HEAD_DIM]
            d_s[g, idx, :] = acc[:, HEAD_DIM:]

        for k in range(n_sets + 1):
            for uu in range(unroll):
                if k < n_sets:
                    scores(*blocks[k * unroll + uu], (k % 2) * unroll + uu)
                if k > 0:
                    values(*blocks[(k - 1) * unroll + uu], ((k - 1) % 2) * unroll + uu)

        def merge(c, c2):
            rows = pl.ds(pl.multiple_of(c * mc, mc), mc)
            def staged(ref, g):
                dil = DILATED_GROUPS[g][1]
                pitch = _row_pitch(dil)
                if pitch == dil:
                    return ref[g, rows, :]
                return jnp.concatenate(
                    [ref[g, pl.ds(pl.multiple_of((c * (mc // dil) + i) * pitch, SUBLANES), dil), :]
                     for i in range(mc // dil)], axis=0)

            m1, m2, m3 = staged(m_s, 0), staged(m_s, 1), staged(m_s, 2)
            mx = jnp.maximum(jnp.maximum(m1, m2), m3)
            w1, w2, w3 = jnp.exp2(m1 - mx), jnp.exp2(m2 - mx), jnp.exp2(m3 - mx)
            num = w1 * staged(o_s, 0) + w2 * staged(o_s, 1) + w3 * staged(o_s, 2)
            den = w1 * staged(d_s, 0) + w2 * staged(d_s, 1) + w3 * staged(d_s, 2)
            orow = pl.ds(pl.multiple_of(T * tile + c * mc, mc), mc)
            y_ref[0, 0, orow, :] = ((num / den) * sz_ref[0, 0, 0, orow, :]).astype(y_ref.dtype)
            return c2

        lax.fori_loop(0, tile // mc, merge, 0)

    for T in range(S // tile):
        tile_body(T)


def _attention(rel_bias, qs, kvs, sz, mc=512, unroll=4, tile=2048):
    B, H, _, S, Dh = qs[0].shape
    bkt = _bucket_tables()
    rows = max(tile // dil * _row_pitch(dil) for _, dil in DILATED_GROUPS)
    hm = lambda parts: pl.BlockSpec((1, 1, parts, S, Dh), lambda h, b: (b, h, 0, 0, 0))
    in_specs = ([pl.BlockSpec(memory_space=pltpu.SMEM),
                 pl.BlockSpec(bkt.shape, lambda h, b: (0, 0, 0))]
                + [hm(1)] * 3 + [hm(2)] * 3 + [hm(1)])
    return pl.pallas_call(
        functools.partial(_attn_kernel, S=S, mc=mc, unroll=unroll, tile=tile),
        grid=(H, B),
        in_specs=in_specs,
        out_specs=pl.BlockSpec((1, 1, S, Dh), lambda h, b: (b, h, 0, 0)),
        out_shape=jax.ShapeDtypeStruct((B, H, S, Dh), BF16),
        scratch_shapes=[pltpu.VMEM((N_GROUPS, BLOCK, 2 * BLOCK), F32),
                        pltpu.VMEM((N_GROUPS, rows, Dh), F32),
                        pltpu.VMEM((N_GROUPS, rows, Dh), F32),
                        pltpu.VMEM((N_GROUPS, rows, Dh), F32),
                        pltpu.VMEM((2 * unroll, BLOCK, 2 * BLOCK), BF16)],
        compiler_params=_params(("arbitrary", "arbitrary")),
        name="attention",
    )(rel_bias.reshape(-1), bkt, *qs, *kvs, sz)


def _final_kernel(y_ref, w_ref, x1_ref, g_ref, o_ref, w_s):
    @pl.when((pl.program_id(0) == 0) & (pl.program_id(1) == 0))
    def _():
        w_s[...] = w_ref[...].astype(BF16)

    y = jnp.concatenate([y_ref[0, h] for h in range(y_ref.shape[1])], axis=-1)
    x2 = x1_ref[0] + jnp.dot(y, w_s[...], preferred_element_type=F32)
    ms = jnp.mean(x2 * x2, axis=-1, keepdims=True)
    o_ref[0] = x2 * lax.rsqrt(ms + EPS) * g_ref[...]


def _final(y, w, x1, g, tm=512):
    B, H, S, Dh = y.shape
    D = w.shape[1]
    return pl.pallas_call(
        _final_kernel,
        grid=(B, S // tm),
        in_specs=[pl.BlockSpec((1, H, tm, Dh), lambda b, i: (b, 0, i, 0)),
                  pl.BlockSpec((H * Dh, D), lambda b, i: (0, 0)),
                  pl.BlockSpec((1, tm, D), lambda b, i: (b, i, 0)),
                  pl.BlockSpec((1, D), lambda b, i: (0, 0))],
        out_specs=pl.BlockSpec((1, tm, D), lambda b, i: (b, i, 0)),
        out_shape=jax.ShapeDtypeStruct((B, S, D), F32),
        scratch_shapes=[pltpu.VMEM((H * Dh, D), BF16)],
        compiler_params=_params(("arbitrary", "arbitrary")),
        name="outproj_final",
    )(y, w, x1, g.reshape(1, D))


def kernel(x, a_norm, a_w_in, a_conv_w, a_conv_b, a_ln_g, a_ln_b, a_w_out, kv_norm, w_kv,
           b_norm, b_w_in, b_w_out, rel_bias, final_norm):
    B, S, D = x.shape
    M = B * S
    u, sz0 = _inproj_a(x.reshape(M, D), a_norm[0], a_w_in[0])
    E = u.shape[-1]
    x1, hk, hb = _conv_out(u.reshape(B, S, E), sz0.reshape(B, S, E), x, a_conv_w[0], a_conv_b[0],
                           a_ln_g[0], a_ln_b[0], a_w_out[0], kv_norm, b_norm[0])
    A = N_HEADS * HEAD_DIM
    qs, kvs = [], []
    for g, (_, dil) in enumerate(DILATED_GROUPS):
        tm = 2048 if dil == 1 else 1024
        qs.append(_proj(hb, b_w_in[0], g * A, A, dil, BF16, "scale", HEAD_DIM ** -0.5 * LOG2E, tm=tm))
        kvs.append(_proj(hk, w_kv, 2 * g * A, 2 * A, dil, BF16, parts=2, tm=tm))
    sz1 = _proj(hb, b_w_in[0], N_GROUPS * A, A, 1, F32, "silu")
    y = _attention(rel_bias, qs, kvs, sz1)
    return _final(y, b_w_out[0], x1, final_norm)
```

```python
import functools
import math

import jax
import jax.numpy as jnp
from jax import lax
from jax.experimental import pallas as pl
from jax.experimental.pallas import tpu as pltpu

D_MODEL = 2048
CONV_WIDTH = 31
HEAD_DIM = 128
N_HEADS = D_MODEL // HEAD_DIM
DILATED_GROUPS = ((128, 1), (512, 4), (2048, 16))
N_GROUPS = len(DILATED_GROUPS)
BLOCK = 128
N_BUCKETS = 32
MAX_EXACT = N_BUCKETS // 2
MAX_DISTANCE = 2048
EPS = 1e-6

LANES = 128
SUBLANES = 8
HALO = 32

F32 = jnp.float32
BF16 = jnp.bfloat16
VMEM_LIMIT = 56 * 1024 * 1024
CONV_VMEM_LIMIT = 60 * 1024 * 1024
LOG2E = math.log2(math.e)


def _silu(x):
    h = 0.5 * x
    return h + h * jnp.tanh(h)


def _params(sem, vmem_limit=VMEM_LIMIT):
    return pltpu.CompilerParams(dimension_semantics=sem, vmem_limit_bytes=vmem_limit)


def _inproj_a_kernel(x_ref, g_ref, wa_ref, wb_ref, wz_ref, u_ref, sz_ref, wa_s, wb_s, wz_s):
    @pl.when(pl.program_id(1) == 0)
    def _():
        wa_s[...] = wa_ref[...].astype(BF16)
        wb_s[...] = wb_ref[...].astype(BF16)
        wz_s[...] = wz_ref[...].astype(BF16)

    x = x_ref[...]
    h = (x * g_ref[...]).astype(BF16)
    rs = lax.rsqrt(jnp.mean(x * x, axis=-1, keepdims=True) + EPS)
    hrs = 0.5 * rs
    ha = jnp.dot(h, wa_s[...], preferred_element_type=F32) * hrs
    hb = jnp.dot(h, wb_s[...], preferred_element_type=F32) * hrs
    u_ref[...] = ha + ha * jnp.tanh(hb)
    hz = jnp.dot(h, wz_s[...], preferred_element_type=F32) * hrs
    sz_ref[...] = hz + hz * jnp.tanh(hz)


def _inproj_a(x2d, g, w_in, tm=512, tn=512):
    M, D = x2d.shape
    E = w_in.shape[1] // 3
    nb = E // tn
    return pl.pallas_call(
        _inproj_a_kernel,
        grid=(nb, M // tm),
        in_specs=[pl.BlockSpec((tm, D), lambda n, m: (m, 0)),
                  pl.BlockSpec((1, D), lambda n, m: (0, 0)),
                  pl.BlockSpec((D, tn), lambda n, m: (0, n)),
                  pl.BlockSpec((D, tn), lambda n, m: (0, n + nb)),
                  pl.BlockSpec((D, tn), lambda n, m: (0, n + 2 * nb))],
        out_specs=[pl.BlockSpec((tm, tn), lambda n, m: (m, n)),
                   pl.BlockSpec((tm, tn), lambda n, m: (m, n))],
        out_shape=[jax.ShapeDtypeStruct((M, E), F32),
                   jax.ShapeDtypeStruct((M, E), F32)],
        scratch_shapes=[pltpu.VMEM((D, tn), BF16)] * 3,
        compiler_params=_params(("arbitrary", "arbitrary")),
        name="inproj_a",
    )(x2d, g.reshape(1, D), w_in, w_in, w_in)


def _fir(load, tap, ntaps, m0, count):
    taps = [tap(i) for i in range(ntaps)]
    accs = [None] * count
    for n in range(m0 - (ntaps - 1), m0 + count):
        x = load(n)
        for p in range(count):
            i = m0 + p - n
            if 0 <= i < ntaps:
                t = x * taps[i]
                accs[p] = t if accs[p] is None else accs[p] + t
    return accs


def _conv_out_kernel(ucur_ref, uprev_ref, sz_ref, x_ref, cw_ref, cb_ref, lg_ref, lb_ref,
                     wo_ref, gkv_ref, gb_ref, x1_ref, hk_ref, hb_ref, seg_s, c_s, s_s, g_s, wo_s,
                     *, tm, pitch, cpitch, oc):
    E = ucur_ref.shape[-1]
    A = tm // SUBLANES
    first = pl.program_id(1) == 0
    n0, n1 = (CONV_WIDTH + 1) // 2, CONV_WIDTH // 2
    hp = HALO // 2

    @pl.when((pl.program_id(0) == 0) & first)
    def _():
        wo_s[...] = wo_ref[...].astype(BF16)
        for ct in range(E // LANES):
            for i in range(n0):
                j = CONV_WIDTH - 1 - 2 * i
                t = cw_ref[ct, j:j + 1, :]
                g_s[ct, i:i + 1, :] = t + cw_ref[ct, j - 1:j, :] if i < n1 else t

    for ct in range(E // LANES):
        cs = slice(ct * LANES, (ct + 1) * LANES)
        seg_s[ct, 0:HALO, :] = jnp.where(first, 0.0, uprev_ref[0, :, cs])
        seg_s[ct, HALO:HALO + A, :] = ucur_ref[0, 0:A, cs]
        for i in range(1, SUBLANES):
            seg_s[ct, i * pitch:i * pitch + HALO + A, :] = ucur_ref[0, i * A - HALO:(i + 1) * A, cs]
        bias = cb_ref[ct, pl.ds(0, SUBLANES, stride=0), :]
        g0 = lambda i: cw_ref[ct, pl.ds(CONV_WIDTH - 1 - 2 * i, SUBLANES, stride=0), :]
        g1 = lambda i: cw_ref[ct, pl.ds(CONV_WIDTH - 2 - 2 * i, SUBLANES, stride=0), :]
        gs = lambda i: g_s[ct, pl.ds(i, SUBLANES, stride=0), :]
        slot = ct % 2
        ev = lambda n: seg_s[ct, pl.ds(HALO + 2 * n, SUBLANES, stride=pitch), :]
        od = lambda n: seg_s[ct, pl.ds(HALO + 2 * n + 1, SUBLANES, stride=pitch), :]
        sm = lambda n: s_s[slot, (n + hp) * SUBLANES:(n + hp + 1) * SUBLANES, :]
        for m0 in range(0, A // 2, oc):
            bv = _fir(od, g1, n1, m0, oc)
            if m0 == 0:
                bprev = _fir(od, g1, n1, -1, 1)[0]
                for n in range(1 - n0, A // 2):
                    s_s[slot, (n + hp) * SUBLANES:(n + hp + 1) * SUBLANES, :] = ev(n) + od(n)
            av = _fir(ev, g0, n0, m0, oc)
            dv = []
            for p in range(oc):
                c_s[ct, pl.ds(2 * (m0 + p), SUBLANES, stride=cpitch), :] = av[p] + bprev + bias
                dv.append(av[p] + bv[p] - bias)
                bprev = bv[p]
            cv = _fir(sm, gs, n0, m0, oc)
            for p in range(oc):
                c_s[ct, pl.ds(2 * (m0 + p) + 1, SUBLANES, stride=cpitch), :] = cv[p] - dv[p]

    nct = E // LANES
    cp = [jnp.concatenate([c_s[ct, i * cpitch:i * cpitch + A, :] for i in range(SUBLANES)], axis=0)
          for ct in range(nct)]
    tot = cp[0]
    for ct in range(1, nct):
        tot = tot + cp[ct]
    mu = jnp.sum(tot, axis=-1, keepdims=True) * (1.0 / E)
    sq = None
    for ct in range(nct):
        d = cp[ct] - mu
        sq = d * d if sq is None else sq + d * d
    rs = lax.rsqrt(jnp.sum(sq, axis=-1, keepdims=True) * (1.0 / E) + EPS)
    ys = []
    for ct in range(nct):
        cs = slice(ct * LANES, (ct + 1) * LANES)
        h = (cp[ct] - mu) * rs * (0.5 * lg_ref[:, cs]) + 0.5 * lb_ref[:, cs]
        ys.append(((h + h * jnp.tanh(h)) * sz_ref[0, :, cs]).astype(BF16))
    y = jnp.concatenate(ys, axis=-1)
    o = jnp.dot(y, wo_s[...], preferred_element_type=F32)
    x1 = x_ref[0] + o
    x1_ref[0] = x1
    xn = x1 * lax.rsqrt(jnp.mean(x1 * x1, axis=-1, keepdims=True) + EPS)
    hk_ref[0] = (xn * gkv_ref[...]).astype(BF16)
    hb_ref[0] = (xn * gb_ref[...]).astype(BF16)


def _conv_out(u, sz, x, conv_w, conv_b, ln_g, ln_b, w_out, g_kv, g_b, tm=256, oc=16):
    B, S, E = u.shape
    D = x.shape[-1]
    nct = E // LANES
    pitch = HALO + tm // SUBLANES + SUBLANES
    cpitch = tm // SUBLANES + SUBLANES
    row = lambda a: a.reshape(1, -1)
    full = lambda shape: pl.BlockSpec(shape, lambda b, i: (0,) * len(shape))
    tile = lambda w: pl.BlockSpec((1, tm, w), lambda b, i: (b, i, 0))
    hpb = tm // HALO
    return pl.pallas_call(
        functools.partial(_conv_out_kernel, tm=tm, pitch=pitch, cpitch=cpitch, oc=oc),
        grid=(B, S // tm),
        in_specs=[tile(E),
                  pl.BlockSpec((1, HALO, E), lambda b, i: (b, jnp.maximum(i * hpb - 1, 0), 0)),
                  tile(E), tile(D),
                  full((nct, CONV_WIDTH, LANES)), full((nct, 1, LANES)), full((1, E)), full((1, E)),
                  full((E, D)), full((1, D)), full((1, D))],
        out_specs=[tile(D), tile(D), tile(D)],
        out_shape=[jax.ShapeDtypeStruct((B, S, D), F32),
                   jax.ShapeDtypeStruct((B, S, D), BF16),
                   jax.ShapeDtypeStruct((B, S, D), BF16)],
        scratch_shapes=[pltpu.VMEM((nct, SUBLANES * pitch, LANES), F32),
                        pltpu.VMEM((nct, SUBLANES * cpitch, LANES), F32),
                        pltpu.VMEM((2, (HALO + tm // SUBLANES) // 2 * SUBLANES, LANES), F32),
                        pltpu.VMEM((nct, (CONV_WIDTH + 1) // 2, LANES), F32),
                        pltpu.VMEM((E, D), BF16)],
        compiler_params=_params(("arbitrary", "arbitrary"), CONV_VMEM_LIMIT),
        name="conv_out",
    )(u, u, sz, x, conv_w.reshape(CONV_WIDTH, nct, LANES).transpose(1, 0, 2),
      conv_b.reshape(nct, 1, LANES), row(ln_g), row(ln_b), w_out, row(g_kv), row(g_b))


def _row_pitch(dil):
    return dil + SUBLANES if dil % (2 * SUBLANES) == 0 else dil


def _proj_kernel(x_ref, w_ref, o_ref, w_s, acc_s, *, dil, epilogue, scale):
    first = (pl.program_id(1) == 0) & (pl.program_id(2) == 0)

    @pl.when(first)
    def _():
        w_s[...] = w_ref[...].astype(BF16)

    acc = jnp.dot(x_ref[0], w_s[...], preferred_element_type=F32)
    if epilogue == "silu":
        acc = _silu(acc)
    elif epilogue == "scale":
        acc = acc * scale
    nh = o_ref.shape[1]
    tl = o_ref.shape[4]
    if dil == 1:
        for hh in range(nh):
            o_ref[0, hh, 0, 0] = acc[:, hh * HEAD_DIM:(hh + 1) * HEAD_DIM].astype(o_ref.dtype)
    else:
        pitch = _row_pitch(dil)
        for hh in range(nh):
            piece = acc[:, hh * HEAD_DIM:(hh + 1) * HEAD_DIM]
            if pitch == dil:
                acc_s[hh] = piece
            else:
                for l in range(tl):
                    acc_s[hh, l * pitch:l * pitch + dil, :] = piece[l * dil:(l + 1) * dil, :]
        for hh in range(nh):
            for r in range(dil):
                o_ref[0, hh, 0, r] = acc_s[hh, pl.ds(r, tl, stride=pitch), :].astype(o_ref.dtype)


def _proj(act, w, col_off, n_cols, dil, out_dtype, epilogue="none", scale=1.0, parts=1, tn=1024, tm=1024):
    B, S, D = act.shape
    L = S // dil
    tl = tm // dil
    nh = tn // HEAD_DIM
    cb0 = col_off // tn
    H = n_cols // parts // HEAD_DIM
    npp = H // nh
    out = pl.pallas_call(
        functools.partial(_proj_kernel, dil=dil, epilogue=epilogue, scale=scale),
        grid=(n_cols // tn, B, S // tm),
        in_specs=[pl.BlockSpec((1, tm, D), lambda n, b, i: (b, i, 0)),
                  pl.BlockSpec((D, tn), lambda n, b, i: (0, cb0 + n))],
        out_specs=pl.BlockSpec((1, nh, 1, dil, tl, HEAD_DIM),
                               lambda n, b, i: (b, n % npp, n // npp, 0, i, 0)),
        out_shape=jax.ShapeDtypeStruct((B, H, parts, dil, L, HEAD_DIM), out_dtype),
        scratch_shapes=[pltpu.VMEM((D, tn), BF16),
                        pltpu.VMEM((nh, tl * _row_pitch(dil), HEAD_DIM), F32)],
        compiler_params=_params(("arbitrary",) * 3),
        name="proj_d%d_%s" % (dil, epilogue),
    )(act, w)
    return out.reshape(B, H, parts, S, HEAD_DIM)


def _bucket_tables():
    delta = (jnp.arange(BLOCK)[:, None] + BLOCK) - jnp.arange(2 * BLOCK)[None, :]
    tabs = []
    for window, dil in DILATED_GROUPS:
        w_sub = window // dil
        local = (delta >= 0) & (delta <= w_sub)
        dist = jnp.clip(delta, 0) * dil
        large = MAX_EXACT + (jnp.log(jnp.maximum(dist, 1).astype(F32) / MAX_EXACT)
                             / math.log(MAX_DISTANCE / MAX_EXACT)
                             * (N_BUCKETS - MAX_EXACT)).astype(jnp.int32)
        large = jnp.minimum(large, N_BUCKETS - 1)
        bucket = jnp.where(dist < MAX_EXACT, dist, large)
        tabs.append(jnp.where(local, bucket, -1).astype(jnp.int32))
    return jnp.stack(tabs)


def _attn_kernel(rb_ref, bkt_ref, q1, q2, q3, kv1, kv2, kv3, sz_ref, y_ref,
                 tbl_s, o_s, m_s, d_s, e_s, *, S, mc, unroll, tile):
    h = pl.program_id(0)

    @pl.when(pl.program_id(1) == 0)
    def _():
        for g in range(N_GROUPS):
            bk = bkt_ref[g]
            t = jnp.full(bk.shape, -jnp.inf, F32)
            for kk in range(N_BUCKETS):
                t = jnp.where(bk == kk, rb_ref[kk * N_HEADS + h] * LOG2E, t)
            tbl_s[g] = t

    groups = ((q1, kv1), (q2, kv2), (q3, kv3))
    bpt = tile // BLOCK

    blocks = [(g, j) for g in range(N_GROUPS) for j in range(bpt)]
    n_sets = len(blocks) // unroll

    def tile_body(T):
        def geometry(g, j):
            dil = DILATED_GROUPS[g][1]
            nb = S // BLOCK // dil
            bpr = bpt // dil
            r, jl = j // bpr, j % bpr
            lb = T * bpr + jl
            p = r * nb + lb
            nk = 1 if lb == 0 else 2
            row = p * BLOCK
            krow = (p + 1 - nk) * BLOCK
            pitch = _row_pitch(dil)
            t0 = jl * (BLOCK * pitch) + r
            idx = pl.ds(t0, BLOCK) if dil == 1 else pl.ds(t0, BLOCK, stride=pitch)
            return row, krow, nk, idx

        def scores(g, j, slot):
            row, krow, nk, idx = geometry(g, j)
            q = groups[g][0][0, 0, 0, pl.ds(row, BLOCK), :]
            k = groups[g][1][0, 0, 0, pl.ds(krow, nk * BLOCK), :]
            s = lax.dot_general(q, k, (((1,), (1,)), ((), ())), preferred_element_type=F32)
            s = s + tbl_s[g, :, (2 - nk) * BLOCK:]
            m = jnp.max(s, axis=-1, keepdims=True)
            e_s[slot, :, 0:nk * BLOCK] = jnp.exp2(s - m).astype(BF16)
            m_s[g, idx, :] = jnp.broadcast_to(m, (BLOCK, HEAD_DIM))

        def values(g, j, slot):
            row, krow, nk, idx = geometry(g, j)
            v = groups[g][1][0, 0, 1, pl.ds(krow, nk * BLOCK), :]
            va = jnp.concatenate([v, jnp.ones_like(v)], axis=1)
            acc = jnp.dot(e_s[slot, :, 0:nk * BLOCK], va, preferred_element_type=F32)
            o_s[g, idx, :] = acc[:, :HEAD_DIM]
            d_s[g, idx, :] = acc[:, HEAD_DIM:]

        for k in range(n_sets + 1):
            for uu in range(unroll):
                if k < n_sets:
                    scores(*blocks[k * unroll + uu], (k % 2) * unroll + uu)
                if k > 0:
                    values(*blocks[(k - 1) * unroll + uu], ((k - 1) % 2) * unroll + uu)

        def merge(c, c2):
            rows = pl.ds(pl.multiple_of(c * mc, mc), mc)
            def staged(ref, g):
                dil = DILATED_GROUPS[g][1]
                pitch = _row_pitch(dil)
                if pitch == dil:
                    return ref[g, rows, :]
                return jnp.concatenate(
                    [ref[g, pl.ds(pl.multiple_of((c * (mc // dil) + i) * pitch, SUBLANES), dil), :]
                     for i in range(mc // dil)], axis=0)

            m1, m2, m3 = staged(m_s, 0), staged(m_s, 1), staged(m_s, 2)
            mx = jnp.maximum(jnp.maximum(m1, m2), m3)
            w1, w2, w3 = jnp.exp2(m1 - mx), jnp.exp2(m2 - mx), jnp.exp2(m3 - mx)
            num = w1 * staged(o_s, 0) + w2 * staged(o_s, 1) + w3 * staged(o_s, 2)
            den = w1 * staged(d_s, 0) + w2 * staged(d_s, 1) + w3 * staged(d_s, 2)
            orow = pl.ds(pl.multiple_of(T * tile + c * mc, mc), mc)
            y_ref[0, 0, orow, :] = ((num / den) * sz_ref[0, 0, 0, orow, :]).astype(y_ref.dtype)
            return c2

        lax.fori_loop(0, tile // mc, merge, 0)

    for T in range(S // tile):
        tile_body(T)


def _attention(rel_bias, qs, kvs, sz, mc=512, unroll=4, tile=2048):
    B, H, _, S, Dh = qs[0].shape
    bkt = _bucket_tables()
    rows = max(tile // dil * _row_pitch(dil) for _, dil in DILATED_GROUPS)
    hm = lambda parts: pl.BlockSpec((1, 1, parts, S, Dh), lambda h, b: (b, h, 0, 0, 0))
    in_specs = ([pl.BlockSpec(memory_space=pltpu.SMEM),
                 pl.BlockSpec(bkt.shape, lambda h, b: (0, 0, 0))]
                + [hm(1)] * 3 + [hm(2)] * 3 + [hm(1)])
    return pl.pallas_call(
        functools.partial(_attn_kernel, S=S, mc=mc, unroll=unroll, tile=tile),
        grid=(H, B),
        in_specs=in_specs,
        out_specs=pl.BlockSpec((1, 1, S, Dh), lambda h, b: (b, h, 0, 0)),
        out_shape=jax.ShapeDtypeStruct((B, H, S, Dh), BF16),
        scratch_shapes=[pltpu.VMEM((N_GROUPS, BLOCK, 2 * BLOCK), F32),
                        pltpu.VMEM((N_GROUPS, rows, Dh), F32),
                        pltpu.VMEM((N_GROUPS, rows, Dh), F32),
                        pltpu.VMEM((N_GROUPS, rows, Dh), F32),
                        pltpu.VMEM((2 * unroll, BLOCK, 2 * BLOCK), BF16)],
        compiler_params=_params(("arbitrary", "arbitrary")),
        name="attention",
    )(rel_bias.reshape(-1), bkt, *qs, *kvs, sz)


def _final_kernel(y_ref, w_ref, x1_ref, g_ref, o_ref, w_s):
    @pl.when((pl.program_id(0) == 0) & (pl.program_id(1) == 0))
    def _():
        w_s[...] = w_ref[...].astype(BF16)

    y = jnp.concatenate([y_ref[0, h] for h in range(y_ref.shape[1])], axis=-1)
    x2 = x1_ref[0] + jnp.dot(y, w_s[...], preferred_element_type=F32)
    ms = jnp.mean(x2 * x2, axis=-1, keepdims=True)
    o_ref[0] = x2 * lax.rsqrt(ms + EPS) * g_ref[...]


def _final(y, w, x1, g, tm=512):
    B, H, S, Dh = y.shape
    D = w.shape[1]
    return pl.pallas_call(
        _final_kernel,
        grid=(B, S // tm),
        in_specs=[pl.BlockSpec((1, H, tm, Dh), lambda b, i: (b, 0, i, 0)),
                  pl.BlockSpec((H * Dh, D), lambda b, i: (0, 0)),
                  pl.BlockSpec((1, tm, D), lambda b, i: (b, i, 0)),
                  pl.BlockSpec((1, D), lambda b, i: (0, 0))],
        out_specs=pl.BlockSpec((1, tm, D), lambda b, i: (b, i, 0)),
        out_shape=jax.ShapeDtypeStruct((B, S, D), F32),
        scratch_shapes=[pltpu.VMEM((H * Dh, D), BF16)],
        compiler_params=_params(("arbitrary", "arbitrary")),
        name="outproj_final",
    )(y, w, x1, g.reshape(1, D))


def kernel(x, a_norm, a_w_in, a_conv_w, a_conv_b, a_ln_g, a_ln_b, a_w_out, kv_norm, w_kv,
           b_norm, b_w_in, b_w_out, rel_bias, final_norm):
    B, S, D = x.shape
    M = B * S
    u, sz0 = _inproj_a(x.reshape(M, D), a_norm[0], a_w_in[0])
    E = u.shape[-1]
    x1, hk, hb = _conv_out(u.reshape(B, S, E), sz0.reshape(B, S, E), x, a_conv_w[0], a_conv_b[0],
                           a_ln_g[0], a_ln_b[0], a_w_out[0], kv_norm, b_norm[0])
    A = N_HEADS * HEAD_DIM
    qs, kvs = [], []
    for g, (_, dil) in enumerate(DILATED_GROUPS):
        tm = 2048 if dil == 1 else 1024
        qs.append(_proj(hb, b_w_in[0], g * A, A, dil, BF16, "scale", HEAD_DIM ** -0.5 * LOG2E, tm=tm))
        kvs.append(_proj(hk, w_kv, 2 * g * A, 2 * A, dil, BF16, parts=2, tm=tm))
    sz1 = _proj(hb, b_w_in[0], N_GROUPS * A, A, 1, F32, "silu")
    y = _attention(rel_bias, qs, kvs, sz1)
    return _final(y, b_w_out[0], x1, final_norm)
```
